```python
import jax, jax.numpy as jnp
from jax import lax
import numpy as np

D_MODEL = 1024
BATCH = 1
SEQ = 16384
DEPTH = 2

HEAD_DIM = 64
N_ATTN_HEADS = 8
N_KV_GROUPS = 2
GQA_RATIO = N_ATTN_HEADS // N_KV_GROUPS
ATTN_WIDTH = N_ATTN_HEADS * HEAD_DIM
N_MLP_GROUPS = 8
MLP_WIDTH = N_MLP_GROUPS * HEAD_DIM
MIX_WIDTH = ATTN_WIDTH + MLP_WIDTH
KV_WIDTH = N_KV_GROUPS * HEAD_DIM
N_BRANCH = 3
GATE_WIDTH = N_ATTN_HEADS * N_BRANCH
IN_SIZES = [ATTN_WIDTH] + [KV_WIDTH] * 6 + [GATE_WIDTH, MLP_WIDTH, MLP_WIDTH]
IN_WIDTH = ATTN_WIDTH + 6 * KV_WIDTH + GATE_WIDTH + 2 * MLP_WIDTH
ROT_DIM = HEAD_DIM // 4
ROPE_THETA = 500000.0
CMP_LEN = 32
CMP_STRIDE = 16
CMP_HIDDEN = 256
SEL_LEN = 64
N_SELECT = 16
WINDOW = 512
Q_BLOCK = 128
CHUNK = 128
D_FF = 2816
CONV_WIDTH = 3
PLE_DIM = 256
NORM_EPS = 1e-6
NEG = -1e30
FORCED = 1e9

kernel_name = "hybrid_nsa_gmlp_convffn_trunk"


def rms_norm(x, gain):
    xf = x.astype(jnp.float32)
    y = xf * lax.rsqrt(jnp.mean(xf * xf, axis=-1, keepdims=True) + NORM_EPS)
    return (y * gain.astype(jnp.float32)).astype(x.dtype)


def layer_norm(x, gain, bias):
    xf = x.astype(jnp.float32)
    mu = jnp.mean(xf, axis=-1, keepdims=True)
    var = jnp.mean(jnp.square(xf - mu), axis=-1, keepdims=True)
    y = (xf - mu) * lax.rsqrt(var + NORM_EPS) * gain.astype(jnp.float32) + bias.astype(jnp.float32)
    return y.astype(x.dtype)


def rope_partial(x, pos):
    half = ROT_DIM // 2
    inv = ROPE_THETA ** (-jnp.arange(half, dtype=jnp.float32) / half)
    ang = pos.astype(jnp.float32)[:, None] * inv[None, :]
    cos = jnp.cos(ang)[:, None, :]
    sin = jnp.sin(ang)[:, None, :]
    xr = x[..., :ROT_DIM].astype(jnp.float32)
    x1, x2 = xr[..., :half], xr[..., half:]
    rot = jnp.concatenate([x1 * cos - x2 * sin, x2 * cos + x1 * sin], axis=-1).astype(x.dtype)
    return jnp.concatenate([rot, x[..., ROT_DIM:]], axis=-1)


def masked_softmax(s, mask):
    s = jnp.where(mask, s, NEG)
    m = jnp.max(s, axis=-1, keepdims=True)
    e = jnp.exp(s - m) * mask
    return e / jnp.maximum(jnp.sum(e, axis=-1, keepdims=True), 1e-30)


def compress(t, pe, w1, b1, w2, b2):
    B, S, G, D = t.shape
    r = CMP_LEN // CMP_STRIDE
    nch = S // CMP_STRIDE
    nc = nch - r + 1
    ch = t.reshape(B, nch, CMP_STRIDE, G, D)
    blocks = jnp.concatenate([ch[:, j:j + nc] for j in range(r)], axis=2)
    blocks = blocks + pe[None, None, :, None, :]
    flat = blocks.transpose(0, 1, 3, 2, 4).reshape(B, nc, G, CMP_LEN * D)
    h = jax.nn.gelu(flat @ w1 + b1)
    return h @ w2 + b2


def nsa_attention(q, kc, vc, ks, vs, kw, vw, gates):
    B, S, H, D = q.shape
    G = kc.shape[2]
    R = H // G
    nc = kc.shape[1]
    ns = S // SEL_LEN
    n_sel = min(N_SELECT, ns)
    scale = D ** -0.5
    cmp_start = jnp.arange(nc) * CMP_STRIDE
    cmp_end = cmp_start + CMP_LEN - 1
    sel_start = jnp.arange(ns) * SEL_LEN
    overlap = ((cmp_end[:, None] >= sel_start[None, :]) &
               (cmp_start[:, None] <= sel_start[None, :] + SEL_LEN - 1)).astype(jnp.float32)
    ks_blk = ks.reshape(B, ns, SEL_LEN, G, D).transpose(0, 3, 1, 2, 4)
    vs_blk = vs.reshape(B, ns, SEL_LEN, G, D).transpose(0, 3, 1, 2, 4)
    kw_pad = jnp.pad(kw, ((0, 0), (WINDOW, 0), (0, 0), (0, 0)))
    vw_pad = jnp.pad(vw, ((0, 0), (WINDOW, 0), (0, 0), (0, 0)))
    qg = q.reshape(B, S, G, R, D)
    gg = gates.reshape(B, S, G, R, N_BRANCH)
    bi = jnp.arange(B)[:, None, None, None]
    gi = jnp.arange(G)[None, :, None, None]
    blk = jnp.arange(ns)
    f32 = jnp.float32

    def block(qi):
        t0 = qi * Q_BLOCK
        tq = t0 + jnp.arange(Q_BLOCK)
        qb = lax.dynamic_slice_in_dim(qg, t0, Q_BLOCK, axis=1)
        gb = lax.dynamic_slice_in_dim(gg, t0, Q_BLOCK, axis=1)
        s_c = jnp.einsum('bqgrd,bngd->bgrqn', qb, kc, preferred_element_type=f32) * scale
        p_c = masked_softmax(s_c, cmp_end[None, :] <= tq[:, None])
        o_c = jnp.einsum('bgrqn,bngd->bqgrd', p_c.astype(vc.dtype), vc)
        imp = jnp.einsum('bgrqn,nj->bgqj', p_c, overlap)
        cur = tq // SEL_LEN
        valid = blk[None, :] <= cur[:, None]
        forced = valid & ((blk[None, :] == 0) | (blk[None, :] == cur[:, None]) | (blk[None, :] == cur[:, None] - 1))
        score = jnp.where(forced, FORCED, jnp.where(valid, imp, NEG))
        vals, idx = lax.top_k(score, n_sel)
        sel_ok = vals > NEG * 0.5
        k_g = ks_blk[bi, gi, idx].reshape(B, G, Q_BLOCK, n_sel * SEL_LEN, D)
        v_g = vs_blk[bi, gi, idx].reshape(B, G, Q_BLOCK, n_sel * SEL_LEN, D)
        kpos = idx[..., None] * SEL_LEN + jnp.arange(SEL_LEN)
        mask_s = ((kpos <= tq[:, None, None]) & sel_ok[..., None]).reshape(B, G, Q_BLOCK, n_sel * SEL_LEN)
        s_s = jnp.einsum('bqgrd,bgqkd->bgrqk', qb, k_g, preferred_element_type=f32) * scale
        p_s = masked_softmax(s_s, mask_s[:, :, None])
        o_s = jnp.einsum('bgrqk,bgqkd->bqgrd', p_s.astype(v_g.dtype), v_g)
        kwb = lax.dynamic_slice_in_dim(kw_pad, t0, Q_BLOCK + WINDOW, axis=1)
        vwb = lax.dynamic_slice_in_dim(vw_pad, t0, Q_BLOCK + WINDOW, axis=1)
        kpos_w = t0 - WINDOW + jnp.arange(Q_BLOCK + WINDOW)
        mask_w = ((kpos_w[None, :] <= tq[:, None]) & (kpos_w[None, :] > tq[:, None] - WINDOW)
                  & (kpos_w[None, :] >= 0))
        s_w = jnp.einsum('bqgrd,bkgd->bgrqk', qb, kwb, preferred_element_type=f32) * scale
        p_w = masked_softmax(s_w, mask_w)
        o_w = jnp.einsum('bgrqk,bkgd->bqgrd', p_w.astype(vwb.dtype), vwb)
        o = gb[..., 0:1] * o_c + gb[..., 1:2] * o_s + gb[..., 2:3] * o_w
        return o.reshape(B, Q_BLOCK, H * D)

    out = lax.map(block, jnp.arange(S // Q_BLOCK))
    return out.transpose(1, 0, 2, 3).reshape(B, S, H * D)


def spatial_gating(u, v, ln_g, ln_b, w_s, b_s):
    B, S, _ = u.shape
    u = jax.nn.gelu(u)
    v = layer_norm(jax.nn.gelu(v), ln_g, ln_b)
    vc = v.reshape(B, S // CHUNK, CHUNK, N_MLP_GROUPS, HEAD_DIM)
    causal = jnp.tril(jnp.ones((CHUNK, CHUNK), dtype=bool))
    ws = jnp.where(causal[None], w_s, 0)
    mixed = jnp.einsum('gts,bcsgd->bctgd', ws, vc) + b_s.T[None, None, :, :, None]
    return u * mixed.reshape(B, S, MLP_WIDTH)


def conv_ffn(x, w_up, conv_w, conv_b, w_down):
    S = x.shape[1]
    h = x @ w_up
    hp = jnp.pad(h, ((0, 0), (CONV_WIDTH - 1, 0), (0, 0)))
    hc = conv_b
    for k in range(CONV_WIDTH):
        hc = hc + hp[:, k:k + S] * conv_w[k]
    g, up = jnp.split(hc, 2, axis=-1)
    return (jax.nn.silu(g) * up) @ w_down


def setup_inputs(seed: int = 0) -> dict:
    key = jax.random.key(seed)
    ks = jax.random.split(key, 32)
    L = DEPTH

    def nrm(k, shape, scale):
        return jax.random.normal(k, shape, dtype=jnp.float32) * scale

    def gain(k, shape):
        return 1.0 + nrm(k, shape, 0.05)

    return {
        "x": nrm(ks[0], (BATCH, SEQ, D_MODEL), 1.0),
        "p": nrm(ks[1], (DEPTH, BATCH, SEQ, PLE_DIM), 1.0),
        "pre_mix_g": gain(ks[2], (L, D_MODEL)),
        "w_in": nrm(ks[3], (L, D_MODEL, IN_WIDTH), D_MODEL ** -0.5),
        "cmp_pe": nrm(ks[4], (L, 2, CMP_LEN, HEAD_DIM), 0.1),
        "cmp_w1": nrm(ks[5], (L, 2, CMP_LEN * HEAD_DIM, CMP_HIDDEN), (CMP_LEN * HEAD_DIM) ** -0.5),
        "cmp_b1": nrm(ks[6], (L, 2, CMP_HIDDEN), 0.02),
        "cmp_w2": nrm(ks[7], (L, 2, CMP_HIDDEN, HEAD_DIM), CMP_HIDDEN ** -0.5),
        "cmp_b2": nrm(ks[8], (L, 2, HEAD_DIM), 0.02),
        "gmlp_ln_g": gain(ks[9], (L, MLP_WIDTH)),
        "gmlp_ln_b": nrm(ks[10], (L, MLP_WIDTH), 0.02),
        "gmlp_ws": nrm(ks[11], (L, N_MLP_GROUPS, CHUNK, CHUNK), 0.5 * CHUNK ** -0.5),
        "gmlp_bs": 1.0 + nrm(ks[12], (L, N_MLP_GROUPS, CHUNK), 0.1),
        "attn_out_g": gain(ks[13], (L, ATTN_WIDTH)),
        "mlp_out_g": gain(ks[14], (L, MLP_WIDTH)),
        "w_o": nrm(ks[15], (L, MIX_WIDTH, D_MODEL), MIX_WIDTH ** -0.5),
        "post_mix_g": gain(ks[16], (L, D_MODEL)),
        "pre_ffn_g": gain(ks[17], (L, D_MODEL)),
        "w_up": nrm(ks[18], (L, D_MODEL, 2 * D_FF), D_MODEL ** -0.5),
        "conv_w": nrm(ks[19], (L, CONV_WIDTH, 2 * D_FF), CONV_WIDTH ** -0.5),
        "conv_b": nrm(ks[20], (L, 2 * D_FF), 0.02),
        "w_down": nrm(ks[21], (L, D_FF, D_MODEL), D_FF ** -0.5),
        "post_ffn_g": gain(ks[22], (L, D_MODEL)),
        "ple_norm_g": gain(ks[23], (L, D_MODEL)),
        "w_ple_gate": nrm(ks[24], (L, D_MODEL, D_MODEL), D_MODEL ** -0.5),
        "w_ple_proj": nrm(ks[25], (L, PLE_DIM, D_MODEL), PLE_DIM ** -0.5),
    }


def reference(x, p, pre_mix_g, w_in, cmp_pe, cmp_w1, cmp_b1, cmp_w2, cmp_b2,
              gmlp_ln_g, gmlp_ln_b, gmlp_ws, gmlp_bs, attn_out_g, mlp_out_g, w_o,
              post_mix_g, pre_ffn_g, w_up, conv_w, conv_b, w_down, post_ffn_g,
              ple_norm_g, w_ple_gate, w_ple_proj):
    B, S, _ = x.shape
    pos = jnp.arange(S)
    split_at = [int(o) for o in np.cumsum(IN_SIZES)[:-1]]
    nc = S // CMP_STRIDE - CMP_LEN // CMP_STRIDE + 1
    cmp_pos = jnp.arange(nc) * CMP_STRIDE + CMP_LEN - 1
    h = x
    for i in range(DEPTH):
        a = rms_norm(h, pre_mix_g[i])
        z = a @ w_in[i]
        zq, zkc, zvc, zks, zvs, zkw, zvw, zg, zu, zv = jnp.split(z, split_at, axis=-1)
        kv = lambda t: t.reshape(B, S, N_KV_GROUPS, HEAD_DIM)
        q = rope_partial(zq.reshape(B, S, N_ATTN_HEADS, HEAD_DIM), pos)
        kc = compress(kv(zkc), cmp_pe[i, 0], cmp_w1[i, 0], cmp_b1[i, 0], cmp_w2[i, 0], cmp_b2[i, 0])
        vc = compress(kv(zvc), cmp_pe[i, 1], cmp_w1[i, 1], cmp_b1[i, 1], cmp_w2[i, 1], cmp_b2[i, 1])
        kc = rope_partial(kc, cmp_pos)
        k_sel = rope_partial(kv(zks), pos)
        k_win = rope_partial(kv(zkw), pos)
        gates = jax.nn.sigmoid(zg.reshape(B, S, N_ATTN_HEADS, N_BRANCH))
        attn = nsa_attention(q, kc, vc, k_sel, kv(zvs), k_win, kv(zvw), gates)
        mlp = spatial_gating(zu, zv, gmlp_ln_g[i], gmlp_ln_b[i], gmlp_ws[i], gmlp_bs[i])
        mix = jnp.concatenate([rms_norm(attn, attn_out_g[i]), rms_norm(mlp, mlp_out_g[i])], axis=-1) @ w_o[i]
        h = h + rms_norm(mix, post_mix_g[i])
        f = conv_ffn(rms_norm(h, pre_ffn_g[i]), w_up[i], conv_w[i], conv_b[i], w_down[i])
        h = h + rms_norm(f, post_ffn_g[i])
        gate = jax.nn.sigmoid(rms_norm(h, ple_norm_g[i]) @ w_ple_gate[i])
        h = h + gate * (p[i] @ w_ple_proj[i])
    return h
```

```python
import functools

import numpy as np
import jax
import jax.numpy as jnp
from jax import lax
from jax.experimental import pallas as pl
from jax.experimental.pallas import tpu as pltpu

F32 = jnp.float32
BF16 = jnp.bfloat16

HEAD_DIM = 64
N_HEADS = 8
N_GROUPS = 2
GQA = N_HEADS // N_GROUPS
N_BRANCH = 3
ATTN_WIDTH = N_HEADS * HEAD_DIM
MLP_GROUPS = 8
MLP_WIDTH = MLP_GROUPS * HEAD_DIM
KV_WIDTH = N_GROUPS * HEAD_DIM
ROT_DIM = HEAD_DIM // 4
ROPE_THETA = 500000.0
CMP_LEN = 32
CMP_STRIDE = 16
CMP_HIDDEN = 256
SEL_LEN = 64
N_SELECT = 16
WINDOW = 512
CHUNK = 128
CONV_WIDTH = 3
NORM_EPS = 1e-6
NEG = -1e30
FORCED = 1e9
SCALE = HEAD_DIM ** -0.5

LANES = 128
Q_TILE = 128
ROWS = GQA * Q_TILE
KV_TILE = 512
BIAS_WIN = 128
VMEM_LIMIT = 56 * 1024 * 1024


def _rms(x, gain):
    return x * lax.rsqrt(jnp.mean(x * x, axis=-1, keepdims=True) + NORM_EPS) * gain


def _rope_slab(x, c, sm, sp):
    return x * c + pltpu.roll(x, LANES - ROT_DIM // 2, 1) * sm + pltpu.roll(x, ROT_DIM // 2, 1) * sp


def _softmax0(s, mask):
    s = jnp.where(mask, s, NEG)
    m = jnp.max(s, axis=0, keepdims=True)
    e = jnp.where(mask, jnp.exp(s - m), 0.0)
    return e / jnp.maximum(jnp.sum(e, axis=0, keepdims=True), 1e-30)


def _inproj_kernel(x_ref, g_ref, w_ref, rc_ref, rm_ref, rp_ref, lng_ref, lnb_ref,
                   q_ref, kv_ref, u_ref, v_ref, gate_ref):
    a = _rms(x_ref[...], g_ref[...]).astype(BF16)
    c, sm, sp = rc_ref[...], rm_ref[...], rp_ref[...]
    zq = jnp.dot(a, w_ref[:, 0:ATTN_WIDTH], preferred_element_type=F32)
    for j in range(ATTN_WIDTH // LANES):
        sl = slice(LANES * j, LANES * (j + 1))
        q_ref[:, sl] = (_rope_slab(zq[:, sl], c, sm, sp) * SCALE).astype(BF16)
    o = ATTN_WIDTH
    zkv = jnp.dot(a, w_ref[:, o:o + 6 * KV_WIDTH], preferred_element_type=F32)
    for j in range(6):
        sl = slice(LANES * j, LANES * (j + 1))
        z = zkv[:, sl]
        if j in (2, 4):
            z = _rope_slab(z, c, sm, sp)
        kv_ref[:, sl] = z.astype(BF16)
    o += 6 * KV_WIDTH
    zu = jnp.dot(a, w_ref[:, o:o + MLP_WIDTH], preferred_element_type=F32)
    u_ref[...] = jax.nn.gelu(zu).astype(BF16)
    o += MLP_WIDTH
    zv = jax.nn.gelu(jnp.dot(a, w_ref[:, o:o + MLP_WIDTH], preferred_element_type=F32))
    mu = jnp.mean(zv, axis=-1, keepdims=True)
    d = zv - mu
    var = jnp.mean(d * d, axis=-1, keepdims=True)
    v_ref[...] = (d * lax.rsqrt(var + NORM_EPS) * lng_ref[...] + lnb_ref[...]).astype(BF16)
    o += MLP_WIDTH
    zg = jnp.dot(a, w_ref[:, o:o + LANES], preferred_element_type=F32)
    gate_ref[...] = jax.nn.sigmoid(zg)


def _inproj(h, gain, w, rc, rm, rp, lng, lnb, tm=512):
    S, D = h.shape
    N = w.shape[1]
    row = lambda i: (i, 0)
    fix = lambda i: (0, 0)
    return pl.pallas_call(
        _inproj_kernel,
        grid=(S // tm,),
        in_specs=[pl.BlockSpec((tm, D), row), pl.BlockSpec((1, D), fix), pl.BlockSpec((D, N), fix),
                  pl.BlockSpec((tm, LANES), row), pl.BlockSpec((tm, LANES), row), pl.BlockSpec((tm, LANES), row),
                  pl.BlockSpec((1, MLP_WIDTH), fix), pl.BlockSpec((1, MLP_WIDTH), fix)],
        out_specs=[pl.BlockSpec((tm, ATTN_WIDTH), row), pl.BlockSpec((tm, 6 * KV_WIDTH), row),
                   pl.BlockSpec((tm, MLP_WIDTH), row), pl.BlockSpec((tm, MLP_WIDTH), row),
                   pl.BlockSpec((tm, LANES), row)],
        out_shape=[jax.ShapeDtypeStruct((S, ATTN_WIDTH), BF16), jax.ShapeDtypeStruct((S, 6 * KV_WIDTH), BF16),
                   jax.ShapeDtypeStruct((S, MLP_WIDTH), BF16), jax.ShapeDtypeStruct((S, MLP_WIDTH), BF16),
                   jax.ShapeDtypeStruct((S, LANES), F32)],
        compiler_params=pltpu.CompilerParams(dimension_semantics=("arbitrary",), vmem_limit_bytes=VMEM_LIMIT),
        name="inproj",
    )(h, gain, w, rc, rm, rp, lng, lnb)


def _compress_kernel(ch_ref, w1c_ref, w1_ref, pe_ref, b1_ref, w2_ref, b2_ref, rc_ref, rm_ref, rp_ref, out_ref):
    nch = ch_ref.shape[2]
    c = jnp.dot(ch_ref[0, 0], w1c_ref[0], preferred_element_type=F32)
    cvec = jnp.dot(pe_ref[0], w1_ref[0], preferred_element_type=F32)[0:1] + b1_ref[0]
    h = jax.nn.gelu(c[:, :CMP_HIDDEN] + pltpu.roll(c[:, CMP_HIDDEN:], nch - 1, 0) + cvec)
    o = jnp.dot(h.astype(BF16), w2_ref[0], preferred_element_type=F32) + b2_ref[0]
    out_ref[0, 0] = _rope_slab(o, rc_ref[0], rm_ref[0], rp_ref[0]).astype(BF16)


def _compress(chunks, w1c, w1, pe, b1, w2, b2, rc, rm, rp):
    _, G, nch, cw = chunks.shape
    kv = lambda k, g: (k, 0, 0)
    return pl.pallas_call(
        _compress_kernel,
        grid=(2, G),
        in_specs=[pl.BlockSpec((1, 1, nch, cw), lambda k, g: (k, g, 0, 0)),
                  pl.BlockSpec((1, cw, 2 * CMP_HIDDEN), kv), pl.BlockSpec((1, 2 * cw, CMP_HIDDEN), kv),
                  pl.BlockSpec((1, 8, 2 * cw), kv), pl.BlockSpec((1, 1, CMP_HIDDEN), kv),
                  pl.BlockSpec((1, CMP_HIDDEN, LANES), kv), pl.BlockSpec((1, 1, LANES), kv),
                  pl.BlockSpec((1, nch, LANES), kv), pl.BlockSpec((1, nch, LANES), kv),
                  pl.BlockSpec((1, nch, LANES), kv)],
        out_specs=pl.BlockSpec((1, 1, nch, LANES), lambda k, g: (k, g, 0, 0)),
        out_shape=jax.ShapeDtypeStruct((2, G, nch, LANES), BF16),
        compiler_params=pltpu.CompilerParams(dimension_semantics=("arbitrary", "arbitrary"),
                                             vmem_limit_bytes=VMEM_LIMIT),
        name="compress",
    )(chunks, w1c, w1, pe, b1, w2, b2, rc, rm, rp)


def _attn_kernel(qT_ref, gate_ref, kall_ref, vT_ref, kc_ref, vcT_ref, ovt_ref, out_ref, qaug_ref, *, ns, ncp):
    b = pl.program_id(1)
    t0 = b * Q_TILE
    qT = qT_ref[0, 0]
    zeros = jnp.zeros((HEAD_DIM, ROWS), BF16)
    qwT = jnp.concatenate([qT, zeros], axis=0)
    lane = lax.broadcasted_iota(jnp.int32, (1, ROWS), 1)
    tq = t0 + (lane & (Q_TILE - 1))

    s = jnp.dot(kc_ref[0], qwT, preferred_element_type=F32)
    cend = lax.broadcasted_iota(jnp.int32, (ncp, 1), 0) * CMP_STRIDE + (CMP_LEN - 1)
    p = _softmax0(s, cend <= tq)
    ocT = jnp.dot(vcT_ref[0], p.astype(BF16), preferred_element_type=F32)

    ps = p[:, 0:Q_TILE]
    for r in range(1, GQA):
        ps = ps + p[:, r * Q_TILE:(r + 1) * Q_TILE]
    hi = ps.astype(BF16)
    r1 = ps - hi.astype(F32)
    mid = r1.astype(BF16)
    lo = (r1 - mid.astype(F32)).astype(BF16)
    ovt = ovt_ref[...]
    imp = (jnp.dot(ovt, hi, preferred_element_type=F32) + jnp.dot(ovt, mid, preferred_element_type=F32)
           + jnp.dot(ovt, lo, preferred_element_type=F32))

    blk = lax.broadcasted_iota(jnp.int32, (ns, Q_TILE), 0)
    cur = (t0 + lax.broadcasted_iota(jnp.int32, (ns, Q_TILE), 1)) // SEL_LEN
    valid = blk <= cur
    forced = valid & ((blk == 0) | (blk == cur) | (blk == cur - 1))
    score = jnp.where(forced, FORCED, jnp.where(valid, imp, NEG))
    blk_f = blk.astype(F32)
    sel = jnp.zeros((ns, Q_TILE), F32)
    for _ in range(N_SELECT):
        m = jnp.max(score, axis=0, keepdims=True)
        idx = jnp.min(jnp.where(score == m, blk_f, float(ns)), axis=0, keepdims=True)
        pick = blk_f == idx
        sel = jnp.where(pick & (m > NEG * 0.5), 1.0, sel)
        score = jnp.where(pick, -3e38, score)
    bias = jnp.where(sel > 0.5, 0.0, NEG).astype(BF16)
    for w in range(ns // BIAS_WIN):
        bw = bias[BIAS_WIN * w:BIAS_WIN * (w + 1), :]
        qaug_ref[w, 0:HEAD_DIM, :] = zeros
        qaug_ref[w, HEAD_DIM:2 * HEAD_DIM, :] = qT
        qaug_ref[w, 2 * HEAD_DIM:, :] = jnp.concatenate([bw] * GQA, axis=1)

    def body(t, carry):
        m, l, acc = carry
        start = pl.multiple_of(WINDOW + t * KV_TILE, KV_TILE)
        kt = kall_ref[0, pl.ds(start, KV_TILE), :]
        qa = qaug_ref[t // (BIAS_WIN * SEL_LEN // KV_TILE)]
        s = jnp.dot(kt, qa, preferred_element_type=F32)
        kpos = t * KV_TILE + lax.broadcasted_iota(jnp.int32, (KV_TILE, 1), 0)
        s = jnp.where(kpos <= tq, s, NEG)
        m_new = jnp.maximum(m, jnp.max(s, axis=0, keepdims=True))
        alpha = jnp.exp(m - m_new)
        pp = jnp.exp(s - m_new)
        l = alpha * l + jnp.sum(pp, axis=0, keepdims=True)
        vt = vT_ref[0, HEAD_DIM:2 * HEAD_DIM, pl.ds(start, KV_TILE)]
        acc = alpha * acc + jnp.dot(vt, pp.astype(BF16), preferred_element_type=F32)
        return m_new, l, acc

    n_tiles = (t0 + Q_TILE + KV_TILE - 1) // KV_TILE
    init = (jnp.full((1, ROWS), NEG, F32), jnp.zeros((1, ROWS), F32), jnp.zeros((HEAD_DIM, ROWS), F32))
    _, l, acc = lax.fori_loop(0, n_tiles, body, init)
    osT = acc / jnp.maximum(l, 1e-30)

    wk = WINDOW + Q_TILE
    start = pl.multiple_of(t0, Q_TILE)
    kw = kall_ref[0, pl.ds(start, wk), 0:LANES]
    s = jnp.dot(kw, qwT, preferred_element_type=F32)
    kpos = t0 - WINDOW + lax.broadcasted_iota(jnp.int32, (wk, 1), 0)
    pw = _softmax0(s, (kpos <= tq) & (kpos > tq - WINDOW) & (kpos >= 0))
    vw = vT_ref[0, 0:HEAD_DIM, pl.ds(start, wk)]
    owT = jnp.dot(vw, pw.astype(BF16), preferred_element_type=F32)

    gt = gate_ref[0, 0]
    out_ref[0, 0] = gt[0:1] * ocT + gt[1:2] * osT + gt[2:3] * owT


def _attention(qT, gates, kall, vT, kc, vcT, ovt):
    G, nqb = qT.shape[0], qT.shape[1]
    sp = kall.shape[1]
    ncp = kc.shape[1]
    ns = ovt.shape[0]
    per_g = lambda g, b: (g, 0, 0)
    per_gb = lambda g, b: (g, b, 0, 0)
    kernel = functools.partial(_attn_kernel, ns=ns, ncp=ncp)
    return pl.pallas_call(
        kernel,
        grid=(G, nqb),
        in_specs=[pl.BlockSpec((1, 1, HEAD_DIM, ROWS), per_gb), pl.BlockSpec((1, 1, N_BRANCH, ROWS), per_gb),
                  pl.BlockSpec((1, sp, 2 * LANES), per_g), pl.BlockSpec((1, LANES, sp), per_g),
                  pl.BlockSpec((1, ncp, LANES), per_g), pl.BlockSpec((1, HEAD_DIM, ncp), per_g),
                  pl.BlockSpec((ns, ncp), lambda g, b: (0, 0))],
        out_specs=pl.BlockSpec((1, 1, HEAD_DIM, ROWS), per_gb),
        out_shape=jax.ShapeDtypeStruct((G, nqb, HEAD_DIM, ROWS), F32),
        scratch_shapes=[pltpu.VMEM((ns // BIAS_WIN, 2 * LANES, ROWS), BF16)],
        compiler_params=pltpu.CompilerParams(dimension_semantics=("arbitrary", "arbitrary"),
                                             vmem_limit_bytes=VMEM_LIMIT),
        name="nsa_attention",
    )(qT, gates, kall, vT, kc, vcT, ovt)


def _gmlp_kernel(u_ref, v_ref, ws_ref, bs_ref, g_ref, out_ref, wsm_ref, *, chunks):
    @pl.when(pl.program_id(0) == 0)
    def _():
        r = lax.broadcasted_iota(jnp.int32, (CHUNK, CHUNK), 0)
        c = lax.broadcasted_iota(jnp.int32, (CHUNK, CHUNK), 1)
        for g in range(MLP_GROUPS):
            wsm_ref[g] = jnp.where(c <= r, ws_ref[g], 0.0).astype(BF16)

    lane = lax.broadcasted_iota(jnp.int32, (CHUNK, LANES), 1)
    left = lane < HEAD_DIM
    for ci in range(chunks):
        rows = slice(ci * CHUNK, (ci + 1) * CHUNK)
        parts = []
        for pr in range(MLP_GROUPS // 2):
            vp = v_ref[rows, pr * LANES:(pr + 1) * LANES]
            va = jnp.where(left, vp, jnp.zeros_like(vp))
            vb = jnp.where(left, jnp.zeros_like(vp), vp)
            parts.append(jnp.dot(wsm_ref[2 * pr], va, preferred_element_type=F32)
                         + jnp.dot(wsm_ref[2 * pr + 1], vb, preferred_element_type=F32))
        mixed = jnp.concatenate(parts, axis=1) + bs_ref[...]
        y = u_ref[rows, :].astype(F32) * mixed
        out_ref[rows, :] = _rms(y, g_ref[...]).astype(BF16)


def _gmlp(u, v, ws, bs_exp, gain, chunks=4):
    S = u.shape[0]
    tm = CHUNK * chunks
    row = lambda i: (i, 0)
    return pl.pallas_call(
        functools.partial(_gmlp_kernel, chunks=chunks),
        grid=(S // tm,),
        in_specs=[pl.BlockSpec((tm, MLP_WIDTH), row), pl.BlockSpec((tm, MLP_WIDTH), row),
                  pl.BlockSpec((MLP_GROUPS, CHUNK, CHUNK), lambda i: (0, 0, 0)),
                  pl.BlockSpec((CHUNK, MLP_WIDTH), lambda i: (0, 0)),
                  pl.BlockSpec((1, MLP_WIDTH), lambda i: (0, 0))],
        out_specs=pl.BlockSpec((tm, MLP_WIDTH), row),
        out_shape=jax.ShapeDtypeStruct((S, MLP_WIDTH), BF16),
        scratch_shapes=[pltpu.VMEM((MLP_GROUPS, CHUNK, CHUNK), BF16)],
        compiler_params=pltpu.CompilerParams(dimension_semantics=("arbitrary",), vmem_limit_bytes=VMEM_LIMIT),
        name="gmlp",
    )(u, v, ws, bs_exp, gain)


def _outproj_kernel(h_ref, a_ref, m_ref, ag_ref, wa_ref, wm_ref, pg_ref, out_ref):
    an = _rms(a_ref[...], ag_ref[...]).astype(BF16)
    mix = (jnp.dot(an, wa_ref[...], preferred_element_type=F32)
           + jnp.dot(m_ref[...], wm_ref[...], preferred_element_type=F32))
    out_ref[...] = h_ref[...] + _rms(mix, pg_ref[...])


def _outproj(h, attn, mlpn, ag, wa, wm, pg, tm=512):
    S, D = h.shape
    row = lambda i: (i, 0)
    fix = lambda i: (0, 0)
    return pl.pallas_call(
        _outproj_kernel,
        grid=(S // tm,),
        in_specs=[pl.BlockSpec((tm, D), row), pl.BlockSpec((tm, ATTN_WIDTH), row), pl.BlockSpec((tm, MLP_WIDTH), row),
                  pl.BlockSpec((1, ATTN_WIDTH), fix), pl.BlockSpec((ATTN_WIDTH, D), fix),
                  pl.BlockSpec((MLP_WIDTH, D), fix), pl.BlockSpec((1, D), fix)],
        out_specs=pl.BlockSpec((tm, D), row),
        out_shape=jax.ShapeDtypeStruct((S, D), F32),
        compiler_params=pltpu.CompilerParams(dimension_semantics=("arbitrary",), vmem_limit_bytes=VMEM_LIMIT),
        name="outproj",
    )(h, attn, mlpn, ag, wa, wm, pg)


HALO = 16


def _ffn_kernel(h_ref, halo_ref, g_ref, wg_ref, wu_ref, cwg_ref, cwu_ref, cbg_ref, cbu_ref, wd_ref, pg_ref,
                out_ref, xn_ref, acc_ref):
    i = pl.program_id(0)
    c = pl.program_id(1)
    tm = h_ref.shape[0]

    @pl.when(c == 0)
    def _():
        hn = _rms(halo_ref[...], g_ref[...])
        xn_ref[0:HALO, :] = jnp.where(i == 0, 0.0, hn).astype(BF16)
        xn_ref[HALO:, :] = _rms(h_ref[...], g_ref[...]).astype(BF16)
        acc_ref[...] = jnp.zeros_like(acc_ref)

    xn = xn_ref[...]

    def conv(w_ref, cw_ref, cb_ref):
        hh = jnp.dot(xn, w_ref[...], preferred_element_type=F32)
        cw = cw_ref[...]
        y = cb_ref[...] + pltpu.roll(hh, 2, 0) * cw[0:1] + pltpu.roll(hh, 1, 0) * cw[1:2] + hh * cw[2:3]
        return y[HALO:, :]

    gate = conv(wg_ref, cwg_ref, cbg_ref)
    up = conv(wu_ref, cwu_ref, cbu_ref)
    act = (jax.nn.silu(gate) * up).astype(BF16)
    acc_ref[...] += jnp.dot(act, wd_ref[...], preferred_element_type=F32)

    @pl.when(c == pl.num_programs(1) - 1)
    def _():
        out_ref[...] = h_ref[...] + _rms(acc_ref[...], pg_ref[...])


def _ffn(h, gain, w_up, conv_w, conv_b, w_down, pg, tm=512, fc=256):
    S, D = h.shape
    dff = w_down.shape[0]
    nfc = dff // fc
    row = lambda i, c: (i, 0)
    fix = lambda i, c: (0, 0)
    lo = lambda i, c: (0, c)
    hi = lambda i, c: (0, c + nfc)
    halo = lambda i, c: (jnp.maximum(i * (tm // HALO) - 1, 0), 0)
    return pl.pallas_call(
        _ffn_kernel,
        grid=(S // tm, nfc),
        in_specs=[pl.BlockSpec((tm, D), row), pl.BlockSpec((HALO, D), halo), pl.BlockSpec((1, D), fix),
                  pl.BlockSpec((D, fc), lo), pl.BlockSpec((D, fc), hi),
                  pl.BlockSpec((CONV_WIDTH, fc), lo), pl.BlockSpec((CONV_WIDTH, fc), hi),
                  pl.BlockSpec((1, fc), lo), pl.BlockSpec((1, fc), hi),
                  pl.BlockSpec((fc, D), lambda i, c: (c, 0)), pl.BlockSpec((1, D), fix)],
        out_specs=pl.BlockSpec((tm, D), row),
        out_shape=jax.ShapeDtypeStruct((S, D), F32),
        scratch_shapes=[pltpu.VMEM((HALO + tm, D), BF16), pltpu.VMEM((tm, D), F32)],
        compiler_params=pltpu.CompilerParams(dimension_semantics=("arbitrary", "arbitrary"),
                                             vmem_limit_bytes=VMEM_LIMIT),
        name="convffn",
    )(h, h, gain, w_up, w_up, conv_w, conv_w, conv_b, conv_b, w_down, pg)


def _ple_kernel(h_ref, p_ref, g_ref, wg_ref, wp_ref, out_ref):
    h = h_ref[...]
    gate = jax.nn.sigmoid(jnp.dot(_rms(h, g_ref[...]).astype(BF16), wg_ref[...], preferred_element_type=F32))
    proj = jnp.dot(p_ref[...].astype(BF16), wp_ref[...], preferred_element_type=F32)
    out_ref[...] = h + gate * proj


def _ple(h, p, gain, wg, wp, tm=512):
    S, D = h.shape
    P = p.shape[1]
    row = lambda i: (i, 0)
    fix = lambda i: (0, 0)
    return pl.pallas_call(
        _ple_kernel,
        grid=(S // tm,),
        in_specs=[pl.BlockSpec((tm, D), row), pl.BlockSpec((tm, P), row), pl.BlockSpec((1, D), fix),
                  pl.BlockSpec((D, D), fix), pl.BlockSpec((P, D), fix)],
        out_specs=pl.BlockSpec((tm, D), row),
        out_shape=jax.ShapeDtypeStruct((S, D), F32),
        compiler_params=pltpu.CompilerParams(dimension_semantics=("arbitrary",), vmem_limit_bytes=VMEM_LIMIT),
        name="ple",
    )(h, p, gain, wg, wp)


def _rope_tables(pos):
    half = ROT_DIM // 2
    inv = ROPE_THETA ** (-jnp.arange(half, dtype=F32) / half)
    ang = pos.astype(F32)[:, None] * inv[None, :]
    cos, sin = jnp.cos(ang), jnp.sin(ang)
    n = pos.shape[0]
    one = jnp.ones((n, HEAD_DIM - ROT_DIM), F32)
    zero = jnp.zeros((n, HEAD_DIM - ROT_DIM), F32)
    zh = jnp.zeros((n, half), F32)
    c = jnp.concatenate([cos, cos, one], axis=1)
    sm = jnp.concatenate([-sin, zh, zero], axis=1)
    sp = jnp.concatenate([zh, sin, zero], axis=1)
    dup = lambda t: jnp.concatenate([t, t], axis=1)
    return dup(c), dup(sm), dup(sp)


def _overlap_t(ns, ncp, nc):
    i = np.arange(ncp)[None, :]
    j = np.arange(ns)[:, None]
    ov = ((i * CMP_STRIDE + CMP_LEN - 1 >= j * SEL_LEN) & (i * CMP_STRIDE <= j * SEL_LEN + SEL_LEN - 1) & (i < nc))
    return jnp.asarray(ov, dtype=BF16)


def _layer(h, p, prm, tabs):
    S, D = h.shape
    G = N_GROUPS
    nqb = S // Q_TILE
    nch = S // CMP_STRIDE
    rc, rm, rp, crc, crm, crp, ovt, onehot = tabs

    sizes = [ATTN_WIDTH] + [KV_WIDTH] * 6 + [N_HEADS * N_BRANCH, MLP_WIDTH, MLP_WIDTH]
    offs = np.concatenate([[0], np.cumsum(sizes)])
    w_in = prm["w_in"]
    seg = lambda k: w_in[:, offs[k]:offs[k + 1]]
    wg_pad = jnp.pad(seg(7), ((0, 0), (0, LANES - N_HEADS * N_BRANCH)))
    w_cat = jnp.concatenate([seg(0)] + [seg(k) for k in range(1, 7)] + [seg(8), seg(9), wg_pad], axis=1).astype(BF16)

    q, kv, u, v, gates = _inproj(h, prm["pre_mix_g"][None], w_cat, rc, rm, rp,
                                 prm["gmlp_ln_g"][None], prm["gmlp_ln_b"][None])

    craw = jnp.stack([kv[:, 0:KV_WIDTH], kv[:, KV_WIDTH:2 * KV_WIDTH]])
    chunks = craw.reshape(2, S, G, HEAD_DIM).transpose(0, 2, 1, 3).reshape(2, G, nch, CMP_STRIDE * HEAD_DIM)
    w1 = prm["cmp_w1"].astype(BF16)
    half = CMP_STRIDE * HEAD_DIM
    w1c = jnp.concatenate([w1[:, :half], w1[:, half:]], axis=2)
    pe = jnp.broadcast_to(prm["cmp_pe"].reshape(2, 1, CMP_LEN * HEAD_DIM), (2, 8, CMP_LEN * HEAD_DIM)).astype(BF16)
    w2 = jnp.pad(prm["cmp_w2"], ((0, 0), (0, 0), (0, LANES - HEAD_DIM))).astype(BF16)
    b2 = jnp.pad(prm["cmp_b2"], ((0, 0), (0, LANES - HEAD_DIM)))[:, None]
    ident = (jnp.ones_like(crc), jnp.zeros_like(crc), jnp.zeros_like(crc))
    comp = _compress(chunks, w1c, w1, pe, prm["cmp_b1"][:, None], w2, b2,
                     jnp.stack([crc, ident[0]]), jnp.stack([crm, ident[1]]), jnp.stack([crp, ident[2]]))
    kc = comp[0]
    vcT = comp[1][:, :, :HEAD_DIM].transpose(0, 2, 1)

    qT = q.reshape(nqb, Q_TILE, G, GQA, HEAD_DIM).transpose(2, 0, 4, 3, 1).reshape(G, nqb, HEAD_DIM, ROWS)
    gT = (gates[:, :N_HEADS * N_BRANCH].reshape(nqb, Q_TILE, G, GQA, N_BRANCH)
          .transpose(2, 0, 4, 3, 1).reshape(G, nqb, N_BRANCH, ROWS))
    grp = lambda j: kv[:, j * KV_WIDTH:(j + 1) * KV_WIDTH].reshape(S, G, HEAD_DIM).transpose(1, 0, 2)
    ksel, vsel, kwin, vwin = grp(2), grp(3), grp(4), grp(5)
    kall = jnp.concatenate([kwin, ksel, jnp.broadcast_to(onehot[None], (G, S, BIAS_WIN))], axis=2)
    kall = jnp.pad(kall, ((0, 0), (WINDOW, 0), (0, 0)))
    vT = jnp.concatenate([vwin, vsel], axis=2).transpose(0, 2, 1)
    vT = jnp.pad(vT, ((0, 0), (0, 0), (WINDOW, 0)))
    oT = _attention(qT, gT, kall, vT, kc, vcT, ovt)
    attn = oT.reshape(G, nqb, HEAD_DIM, GQA, Q_TILE).transpose(1, 4, 0, 3, 2).reshape(S, ATTN_WIDTH)

    bs_exp = jnp.repeat(prm["gmlp_bs"].T, HEAD_DIM, axis=1)
    mlpn = _gmlp(u, v, prm["gmlp_ws"], bs_exp, prm["mlp_out_g"][None])

    w_o = prm["w_o"].astype(BF16)
    h = _outproj(h, attn, mlpn, prm["attn_out_g"][None], w_o[:ATTN_WIDTH], w_o[ATTN_WIDTH:], prm["post_mix_g"][None])
    h = _ffn(h, prm["pre_ffn_g"][None], prm["w_up"].astype(BF16), prm["conv_w"], prm["conv_b"][None],
             prm["w_down"].astype(BF16), prm["post_ffn_g"][None])
    h = _ple(h, p, prm["ple_norm_g"][None], prm["w_ple_gate"].astype(BF16), prm["w_ple_proj"].astype(BF16))
    return h


def kernel(x, p, pre_mix_g, w_in, cmp_pe, cmp_w1, cmp_b1, cmp_w2, cmp_b2, gmlp_ln_g, gmlp_ln_b, gmlp_ws, gmlp_bs,
           attn_out_g, mlp_out_g, w_o, post_mix_g, pre_ffn_g, w_up, conv_w, conv_b, w_down, post_ffn_g,
           ple_norm_g, w_ple_gate, w_ple_proj):
    B, S, D = x.shape
    depth = p.shape[0]
    assert S % (BIAS_WIN * SEL_LEN) == 0 and D % LANES == 0
    ns = S // SEL_LEN
    nch = S // CMP_STRIDE
    nc = nch - CMP_LEN // CMP_STRIDE + 1
    pos = jnp.arange(S)
    rc, rm, rp = _rope_tables(pos)
    crc, crm, crp = _rope_tables(jnp.arange(nch) * CMP_STRIDE + CMP_LEN - 1)
    onehot = ((pos // SEL_LEN) % BIAS_WIN)[:, None] == jnp.arange(BIAS_WIN)[None, :]
    tabs = (rc, rm, rp, crc, crm, crp, _overlap_t(ns, nch, nc), onehot.astype(BF16))
    stacked = dict(pre_mix_g=pre_mix_g, w_in=w_in, cmp_pe=cmp_pe, cmp_w1=cmp_w1, cmp_b1=cmp_b1, cmp_w2=cmp_w2,
                   cmp_b2=cmp_b2, gmlp_ln_g=gmlp_ln_g, gmlp_ln_b=gmlp_ln_b, gmlp_ws=gmlp_ws, gmlp_bs=gmlp_bs,
                   attn_out_g=attn_out_g, mlp_out_g=mlp_out_g, w_o=w_o, post_mix_g=post_mix_g, pre_ffn_g=pre_ffn_g,
                   w_up=w_up, conv_w=conv_w, conv_b=conv_b, w_down=w_down, post_ffn_g=post_ffn_g,
                   ple_norm_g=ple_norm_g, w_ple_gate=w_ple_gate, w_ple_proj=w_ple_proj)
    outs = []
    for bi in range(B):
        h = x[bi]
        for i in range(depth):
            h = _layer(h, p[i, bi], {k: v[i] for k, v in stacked.items()}, tabs)
        outs.append(h)
    return jnp.stack(outs)
```

```python
import functools
import math

import numpy as np
import jax
import jax.numpy as jnp
from jax import lax
from jax.experimental import pallas as pl
from jax.experimental.pallas import tpu as pltpu

F32 = jnp.float32
BF16 = jnp.bfloat16

HEAD_DIM = 64
N_HEADS = 8
N_GROUPS = 2
GQA = N_HEADS // N_GROUPS
N_BRANCH = 3
ATTN_WIDTH = N_HEADS * HEAD_DIM
MLP_GROUPS = 8
MLP_WIDTH = MLP_GROUPS * HEAD_DIM
KV_WIDTH = N_GROUPS * HEAD_DIM
ROT_DIM = HEAD_DIM // 4
ROPE_THETA = 500000.0
CMP_LEN = 32
CMP_STRIDE = 16
CMP_HIDDEN = 256
SEL_LEN = 64
N_SELECT = 16
WINDOW = 512
CHUNK = 128
CONV_WIDTH = 3
NORM_EPS = 1e-6
NEG = -1e30
FORCED = 1e9
SCALE = HEAD_DIM ** -0.5
QSCALE = SCALE * math.log2(math.e)

LANES = 128
BF16_ROWS = 16
Q_TILE = 128
ROWS = GQA * Q_TILE
KV_TILE = 1024
BIAS_WIN = 128
V_ROWS = HEAD_DIM + BF16_ROWS
CMP_PER_Q = Q_TILE // CMP_STRIDE
CMP_CLASSES = 4
N_FORCED = 3
SEL_PER_Q = Q_TILE // SEL_LEN
IMP_TAPS = range(-1, SEL_LEN // CMP_STRIDE)
VMEM_LIMIT = 56 * 1024 * 1024


def _rms(x, gain):
    return x * lax.rsqrt(jnp.mean(x * x, axis=-1, keepdims=True) + NORM_EPS) * gain


def _rope_slab(x, c, sm, sp):
    return x * c + pltpu.roll(x, LANES - ROT_DIM // 2, 1) * sm + pltpu.roll(x, ROT_DIM // 2, 1) * sp


def _colmax(s):
    return jnp.max(s, axis=0, keepdims=True)


def _tile_heads(x):
    return jnp.concatenate([x] * GQA, axis=1)


def _inproj_kernel(x_ref, g_ref, w_ref, rc_ref, rm_ref, rp_ref, lng_ref, lnb_ref,
                   q_ref, kv_ref, u_ref, v_ref, gate_ref):
    a = _rms(x_ref[...], g_ref[...]).astype(BF16)
    c, sm, sp = rc_ref[...], rm_ref[...], rp_ref[...]
    zq = jnp.dot(a, w_ref[:, 0:ATTN_WIDTH], preferred_element_type=F32)
    for j in range(ATTN_WIDTH // LANES):
        sl = slice(LANES * j, LANES * (j + 1))
        q_ref[:, sl] = (_rope_slab(zq[:, sl], c, sm, sp) * QSCALE).astype(BF16)
    o = ATTN_WIDTH
    zkv = jnp.dot(a, w_ref[:, o:o + 6 * KV_WIDTH], preferred_element_type=F32)
    for j in range(6):
        sl = slice(LANES * j, LANES * (j + 1))
        z = zkv[:, sl]
        if j in (2, 4):
            z = _rope_slab(z, c, sm, sp)
        kv_ref[:, sl] = z.astype(BF16)
    o += 6 * KV_WIDTH
    zu = jnp.dot(a, w_ref[:, o:o + MLP_WIDTH], preferred_element_type=F32)
    u_ref[...] = jax.nn.gelu(zu).astype(BF16)
    o += MLP_WIDTH
    zv = jax.nn.gelu(jnp.dot(a, w_ref[:, o:o + MLP_WIDTH], preferred_element_type=F32))
    mu = jnp.mean(zv, axis=-1, keepdims=True)
    d = zv - mu
    var = jnp.mean(d * d, axis=-1, keepdims=True)
    v_ref[...] = (d * lax.rsqrt(var + NORM_EPS) * lng_ref[...] + lnb_ref[...]).astype(BF16)
    o += MLP_WIDTH
    zg = jnp.dot(a, w_ref[:, o:o + LANES], preferred_element_type=F32)
    gate_ref[...] = jax.nn.sigmoid(zg)


def _inproj(h, gain, w, rc, rm, rp, lng, lnb, tm=512):
    S, D = h.shape
    N = w.shape[1]
    row = lambda i: (i, 0)
    fix = lambda i: (0, 0)
    return pl.pallas_call(
        _inproj_kernel,
        grid=(S // tm,),
        in_specs=[pl.BlockSpec((tm, D), row), pl.BlockSpec((1, D), fix), pl.BlockSpec((D, N), fix),
                  pl.BlockSpec((tm, LANES), row), pl.BlockSpec((tm, LANES), row), pl.BlockSpec((tm, LANES), row),
                  pl.BlockSpec((1, MLP_WIDTH), fix), pl.BlockSpec((1, MLP_WIDTH), fix)],
        out_specs=[pl.BlockSpec((tm, ATTN_WIDTH), row), pl.BlockSpec((tm, 6 * KV_WIDTH), row),
                   pl.BlockSpec((tm, MLP_WIDTH), row), pl.BlockSpec((tm, MLP_WIDTH), row),
                   pl.BlockSpec((tm, LANES), row)],
        out_shape=[jax.ShapeDtypeStruct((S, ATTN_WIDTH), BF16), jax.ShapeDtypeStruct((S, 6 * KV_WIDTH), BF16),
                   jax.ShapeDtypeStruct((S, MLP_WIDTH), BF16), jax.ShapeDtypeStruct((S, MLP_WIDTH), BF16),
                   jax.ShapeDtypeStruct((S, LANES), F32)],
        compiler_params=pltpu.CompilerParams(dimension_semantics=("arbitrary",), vmem_limit_bytes=VMEM_LIMIT),
        name="inproj",
    )(h, gain, w, rc, rm, rp, lng, lnb)


def _compress_kernel(ch_ref, w1c_ref, w1_ref, pe_ref, b1_ref, w2_ref, b2_ref, rc_ref, rm_ref, rp_ref, out_ref):
    nch = ch_ref.shape[2]
    c = jnp.dot(ch_ref[0, 0], w1c_ref[0], preferred_element_type=F32)
    cvec = jnp.dot(pe_ref[0], w1_ref[0], preferred_element_type=F32)[0:1] + b1_ref[0]
    h = jax.nn.gelu(c[:, :CMP_HIDDEN] + pltpu.roll(c[:, CMP_HIDDEN:], nch - 1, 0) + cvec)
    o = jnp.dot(h.astype(BF16), w2_ref[0], preferred_element_type=F32) + b2_ref[0]
    out_ref[0, 0] = _rope_slab(o, rc_ref[0], rm_ref[0], rp_ref[0]).astype(BF16)


def _compress(chunks, w1c, w1, pe, b1, w2, b2, rc, rm, rp):
    _, G, nch, cw = chunks.shape
    kv = lambda k, g: (k, 0, 0)
    return pl.pallas_call(
        _compress_kernel,
        grid=(2, G),
        in_specs=[pl.BlockSpec((1, 1, nch, cw), lambda k, g: (k, g, 0, 0)),
                  pl.BlockSpec((1, cw, 2 * CMP_HIDDEN), kv), pl.BlockSpec((1, 2 * cw, CMP_HIDDEN), kv),
                  pl.BlockSpec((1, 8, 2 * cw), kv), pl.BlockSpec((1, 1, CMP_HIDDEN), kv),
                  pl.BlockSpec((1, CMP_HIDDEN, LANES), kv), pl.BlockSpec((1, 1, LANES), kv),
                  pl.BlockSpec((1, nch, LANES), kv), pl.BlockSpec((1, nch, LANES), kv),
                  pl.BlockSpec((1, nch, LANES), kv)],
        out_specs=pl.BlockSpec((1, 1, nch, LANES), lambda k, g: (k, g, 0, 0)),
        out_shape=jax.ShapeDtypeStruct((2, G, nch, LANES), BF16),
        compiler_params=pltpu.CompilerParams(dimension_semantics=("arbitrary", "arbitrary"),
                                             vmem_limit_bytes=VMEM_LIMIT),
        name="compress",
    )(chunks, w1c, w1, pe, b1, w2, b2, rc, rm, rp)


def _attn_kernel(qT_ref, gate_ref, ksa_ref, vsa_ref, kwa_ref, vwa_ref, kc_ref, vcT_ref, cbt_ref, wb_ref, lb_ref,
                 out_ref, qaug_ref, sbuf_ref, acc_ref, ma_ref, ms_ref, ps_ref, oc_ref, sel_ref, *, ns, ncp):
    b = pl.program_id(1)
    t0 = pl.multiple_of(b * Q_TILE, Q_TILE)
    qT = qT_ref[0, 0]
    zeros = jnp.zeros((HEAD_DIM, ROWS), BF16)
    q0T = jnp.concatenate([qT, zeros], axis=0)

    ps_ref[0:8, :] = jnp.zeros((8, Q_TILE), F32)
    step = ncp // CMP_CLASSES
    cls = (b * CMP_PER_Q + CMP_PER_Q - 2) // step
    for k in range(CMP_CLASSES):
        @pl.when(cls == k)
        def _(nk=(k + 1) * step):
            cb = cbt_ref[pl.ds(pl.multiple_of(ncp - b * CMP_PER_Q, 8), nk), :]
            s = jnp.dot(kc_ref[0, 0:nk, :], q0T, preferred_element_type=F32) + _tile_heads(cb)
            m = _colmax(s)
            e = jnp.exp2(s - m)
            inv = jnp.where(m > NEG * 0.5, 1.0 / jnp.maximum(jnp.sum(e, axis=0, keepdims=True), 1e-30), 0.0)
            p = e * inv
            oc_ref[...] = jnp.dot(vcT_ref[0, :, 0:nk], p.astype(BF16), preferred_element_type=F32)
            ps = p[:, 0:Q_TILE]
            for r in range(1, GQA):
                ps = ps + p[:, r * Q_TILE:(r + 1) * Q_TILE]
            ps_ref[8:8 + nk, :] = ps
            if nk < ncp:
                ps_ref[8 + nk:, :] = jnp.zeros((ncp - nk, Q_TILE), F32)

    ratio = SEL_LEN // CMP_STRIDE
    imp = sum(ps_ref[pl.ds(8 + d, ns, stride=ratio), :] for d in IMP_TAPS)

    blk = lax.broadcasted_iota(jnp.int32, (ns, Q_TILE), 0)
    cur = (t0 + lax.broadcasted_iota(jnp.int32, (ns, Q_TILE), 1)) // SEL_LEN
    valid = blk <= cur
    forced = valid & ((blk == 0) | (blk == cur) | (blk == cur - 1))
    rank0 = jnp.where(valid & jnp.logical_not(forced), imp, NEG)
    blk_f = blk.astype(F32)
    taken = -(2.0 ** 126)

    def rank(break_ties):
        score = rank0
        for _ in range(N_SELECT - N_FORCED):
            mx = _colmax(score)
            if break_ties:
                idx = jnp.min(jnp.where(score == mx, blk_f, float(ns)), axis=0, keepdims=True)
                hit = blk_f == idx
            else:
                hit = score == mx
            score = jnp.where(hit, taken, score)
        return (score == taken) & (rank0 > NEG * 0.5)

    picked = rank(False)
    count = jnp.sum(jnp.where(picked, 1.0, 0.0), axis=0, keepdims=True)
    want = jnp.clip(cur[0:1] - (N_FORCED - 1), 0, N_SELECT - N_FORCED).astype(F32)
    sel_ref[...] = jnp.where(forced | picked, 1.0, 0.0)

    @pl.when(jnp.logical_not(jnp.all(count == want)))
    def _():
        sel_ref[...] = jnp.where(forced | rank(True), 1.0, 0.0)

    sel = sel_ref[...] > 0.5
    bias = jnp.where(sel & (blk < b * SEL_PER_Q), 0.0, NEG).astype(BF16)
    for w in range(ns // BIAS_WIN):
        qaug_ref[w, 0:HEAD_DIM, :] = qT
        qaug_ref[w, HEAD_DIM:2 * HEAD_DIM, :] = zeros
        qaug_ref[w, 2 * HEAD_DIM:, :] = _tile_heads(bias[BIAS_WIN * w:BIAS_WIN * (w + 1), :])

    s = (jnp.dot(ksa_ref[0, pl.ds(t0, Q_TILE), 0:LANES], q0T, preferred_element_type=F32)
         + _tile_heads(lb_ref[...]))
    m0 = _colmax(s)
    acc_ref[...] = jnp.dot(vsa_ref[0, :, pl.ds(t0, Q_TILE)], jnp.exp2(s - m0).astype(BF16),
                           preferred_element_type=F32)
    ma_ref[...] = m0

    n_tiles = jnp.maximum((t0 + KV_TILE - 1) // KV_TILE, 1)
    tiles_per_win = BIAS_WIN * SEL_LEN // KV_TILE

    def scores(t, slot):
        start = pl.multiple_of(t * KV_TILE, KV_TILE)
        s = jnp.dot(ksa_ref[0, pl.ds(start, KV_TILE), :], qaug_ref[t // tiles_per_win],
                    preferred_element_type=F32)
        sbuf_ref[slot] = s
        ms_ref[...] = jnp.maximum(ms_ref[...], _colmax(s))

    def consume(t, slot, m_s):
        start = pl.multiple_of(t * KV_TILE, KV_TILE)
        alpha = jnp.exp2(ma_ref[...] - m_s)
        pp = jnp.exp2(sbuf_ref[slot] - m_s).astype(BF16)
        acc_ref[...] = alpha * acc_ref[...] + jnp.dot(vsa_ref[0, :, pl.ds(start, KV_TILE)], pp,
                                                      preferred_element_type=F32)
        ma_ref[...] = m_s

    def step(t, slot):
        m_s = ms_ref[...]
        scores(t + 1, 1 - slot)
        consume(t, slot, m_s)

    ms_ref[...] = m0
    scores(0, 0)

    def body(i, carry):
        step(2 * i, 0)
        step(2 * i + 1, 1)
        return carry

    n_steps = n_tiles - 1
    lax.fori_loop(0, n_steps // 2, body, 0)

    @pl.when(n_steps % 2 == 1)
    def _():
        step(n_steps - 1, 0)

    consume(n_steps, n_steps % 2, ms_ref[...])

    wk = WINDOW + Q_TILE
    flag = jnp.where(lax.broadcasted_iota(jnp.int32, (BF16_ROWS, ROWS), 0) == 0, NEG, 0.0).astype(BF16)
    qwT = jnp.concatenate([qT, flag, jnp.zeros((HEAD_DIM - BF16_ROWS, ROWS), BF16)], axis=0)
    s = jnp.dot(kwa_ref[0, pl.ds(t0, wk), :], qwT, preferred_element_type=F32) + _tile_heads(wb_ref[...])
    pw = jnp.exp2(s - _colmax(s)).astype(BF16)
    ow = jnp.dot(vwa_ref[0, :, pl.ds(t0, wk)], pw, preferred_element_type=F32)
    owT = ow[0:HEAD_DIM] / jnp.maximum(ow[HEAD_DIM:HEAD_DIM + 1], 1e-30)

    acc = acc_ref[...]
    osT = acc[0:HEAD_DIM] / jnp.maximum(acc[HEAD_DIM:HEAD_DIM + 1], 1e-30)

    gt = gate_ref[0, 0]
    out_ref[0, 0] = gt[0:1] * oc_ref[...] + gt[1:2] * osT + gt[2:3] * owT


def _attention(qT, gates, ksa, vsa, kwa, vwa, kc, vcT, cbt, wb, lb):
    G, nqb = qT.shape[0], qT.shape[1]
    S = ksa.shape[1]
    sp = kwa.shape[1]
    ncp = kc.shape[1]
    ns = S // SEL_LEN
    per_g = lambda g, b: (g, 0, 0)
    per_gb = lambda g, b: (g, b, 0, 0)
    fix = lambda g, b: (0, 0)
    kernel = functools.partial(_attn_kernel, ns=ns, ncp=ncp)
    return pl.pallas_call(
        kernel,
        grid=(G, nqb),
        in_specs=[pl.BlockSpec((1, 1, HEAD_DIM, ROWS), per_gb), pl.BlockSpec((1, 1, N_BRANCH, ROWS), per_gb),
                  pl.BlockSpec((1, S, 2 * LANES), per_g), pl.BlockSpec((1, V_ROWS, S), per_g),
                  pl.BlockSpec((1, sp, LANES), per_g), pl.BlockSpec((1, V_ROWS, sp), per_g),
                  pl.BlockSpec((1, ncp, LANES), per_g), pl.BlockSpec((1, HEAD_DIM, ncp), per_g),
                  pl.BlockSpec((2 * ncp, Q_TILE), fix), pl.BlockSpec((WINDOW + Q_TILE, Q_TILE), fix),
                  pl.BlockSpec((Q_TILE, Q_TILE), fix)],
        out_specs=pl.BlockSpec((1, 1, HEAD_DIM, ROWS), per_gb),
        out_shape=jax.ShapeDtypeStruct((G, nqb, HEAD_DIM, ROWS), F32),
        scratch_shapes=[pltpu.VMEM((ns // BIAS_WIN, 2 * LANES, ROWS), BF16),
                        pltpu.VMEM((2, KV_TILE, ROWS), F32),
                        pltpu.VMEM((V_ROWS, ROWS), F32),
                        pltpu.VMEM((1, ROWS), F32), pltpu.VMEM((1, ROWS), F32),
                        pltpu.VMEM((8 + ncp, Q_TILE), F32),
                        pltpu.VMEM((HEAD_DIM, ROWS), F32),
                        pltpu.VMEM((ns, Q_TILE), F32)],
        compiler_params=pltpu.CompilerParams(dimension_semantics=("arbitrary", "arbitrary"),
                                             vmem_limit_bytes=VMEM_LIMIT),
        name="nsa_attention",
    )(qT, gates, ksa, vsa, kwa, vwa, kc, vcT, cbt, wb, lb)


def _gmlp_kernel(u_ref, v_ref, ws_ref, bs_ref, g_ref, out_ref, wsm_ref, *, chunks):
    @pl.when(pl.program_id(0) == 0)
    def _():
        r = lax.broadcasted_iota(jnp.int32, (CHUNK, CHUNK), 0)
        c = lax.broadcasted_iota(jnp.int32, (CHUNK, CHUNK), 1)
        for g in range(MLP_GROUPS):
            wsm_ref[g] = jnp.where(c <= r, ws_ref[g], 0.0).astype(BF16)

    lane = lax.broadcasted_iota(jnp.int32, (CHUNK, LANES), 1)
    left = lane < HEAD_DIM
    for ci in range(chunks):
        rows = slice(ci * CHUNK, (ci + 1) * CHUNK)
        parts = []
        for pr in range(MLP_GROUPS // 2):
            vp = v_ref[rows, pr * LANES:(pr + 1) * LANES]
            va = jnp.where(left, vp, jnp.zeros_like(vp))
            vb = jnp.where(left, jnp.zeros_like(vp), vp)
            parts.append(jnp.dot(wsm_ref[2 * pr], va, preferred_element_type=F32)
                         + jnp.dot(wsm_ref[2 * pr + 1], vb, preferred_element_type=F32))
        mixed = jnp.concatenate(parts, axis=1) + bs_ref[...]
        y = u_ref[rows, :].astype(F32) * mixed
        out_ref[rows, :] = _rms(y, g_ref[...]).astype(BF16)


def _gmlp(u, v, ws, bs_exp, gain, chunks=4):
    S = u.shape[0]
    tm = CHUNK * chunks
    row = lambda i: (i, 0)
    return pl.pallas_call(
        functools.partial(_gmlp_kernel, chunks=chunks),
        grid=(S // tm,),
        in_specs=[pl.BlockSpec((tm, MLP_WIDTH), row), pl.BlockSpec((tm, MLP_WIDTH), row),
                  pl.BlockSpec((MLP_GROUPS, CHUNK, CHUNK), lambda i: (0, 0, 0)),
                  pl.BlockSpec((CHUNK, MLP_WIDTH), lambda i: (0, 0)),
                  pl.BlockSpec((1, MLP_WIDTH), lambda i: (0, 0))],
        out_specs=pl.BlockSpec((tm, MLP_WIDTH), row),
        out_shape=jax.ShapeDtypeStruct((S, MLP_WIDTH), BF16),
        scratch_shapes=[pltpu.VMEM((MLP_GROUPS, CHUNK, CHUNK), BF16)],
        compiler_params=pltpu.CompilerParams(dimension_semantics=("arbitrary",), vmem_limit_bytes=VMEM_LIMIT),
        name="gmlp",
    )(u, v, ws, bs_exp, gain)


def _outproj_kernel(h_ref, a_ref, m_ref, ag_ref, wa_ref, wm_ref, pg_ref, out_ref):
    an = _rms(a_ref[...], ag_ref[...]).astype(BF16)
    mix = (jnp.dot(an, wa_ref[...], preferred_element_type=F32)
           + jnp.dot(m_ref[...], wm_ref[...], preferred_element_type=F32))
    out_ref[...] = h_ref[...] + _rms(mix, pg_ref[...])


def _outproj(h, attn, mlpn, ag, wa, wm, pg, tm=512):
    S, D = h.shape
    row = lambda i: (i, 0)
    fix = lambda i: (0, 0)
    return pl.pallas_call(
        _outproj_kernel,
        grid=(S // tm,),
        in_specs=[pl.BlockSpec((tm, D), row), pl.BlockSpec((tm, ATTN_WIDTH), row), pl.BlockSpec((tm, MLP_WIDTH), row),
                  pl.BlockSpec((1, ATTN_WIDTH), fix), pl.BlockSpec((ATTN_WIDTH, D), fix),
                  pl.BlockSpec((MLP_WIDTH, D), fix), pl.BlockSpec((1, D), fix)],
        out_specs=pl.BlockSpec((tm, D), row),
        out_shape=jax.ShapeDtypeStruct((S, D), F32),
        compiler_params=pltpu.CompilerParams(dimension_semantics=("arbitrary",), vmem_limit_bytes=VMEM_LIMIT),
        name="outproj",
    )(h, attn, mlpn, ag, wa, wm, pg)


HALO = BF16_ROWS


def _ffn_kernel(h_ref, halo_ref, g_ref, wg_ref, wu_ref, cwg_ref, cwu_ref, cbg_ref, cbu_ref, wd_ref, pg_ref,
                out_ref, xn_ref, acc_ref):
    i = pl.program_id(0)
    c = pl.program_id(1)

    @pl.when(c == 0)
    def _():
        hn = _rms(halo_ref[...], g_ref[...])
        xn_ref[0:HALO, :] = jnp.where(i == 0, 0.0, hn).astype(BF16)
        xn_ref[HALO:, :] = _rms(h_ref[...], g_ref[...]).astype(BF16)
        acc_ref[...] = jnp.zeros_like(acc_ref)

    xn = xn_ref[...]

    def conv(w_ref, cw_ref, cb_ref):
        hh = jnp.dot(xn, w_ref[...], preferred_element_type=F32)
        cw = cw_ref[...]
        y = cb_ref[...] + pltpu.roll(hh, 2, 0) * cw[0:1] + pltpu.roll(hh, 1, 0) * cw[1:2] + hh * cw[2:3]
        return y[HALO:, :]

    gate = conv(wg_ref, cwg_ref, cbg_ref)
    up = conv(wu_ref, cwu_ref, cbu_ref)
    act = (jax.nn.silu(gate) * up).astype(BF16)
    acc_ref[...] += jnp.dot(act, wd_ref[...], preferred_element_type=F32)

    @pl.when(c == pl.num_programs(1) - 1)
    def _():
        out_ref[...] = h_ref[...] + _rms(acc_ref[...], pg_ref[...])


def _ffn(h, gain, w_up, conv_w, conv_b, w_down, pg, tm=512, fc=256):
    S, D = h.shape
    dff = w_down.shape[0]
    nfc = dff // fc
    row = lambda i, c: (i, 0)
    fix = lambda i, c: (0, 0)
    lo = lambda i, c: (0, c)
    hi = lambda i, c: (0, c + nfc)
    halo = lambda i, c: (jnp.maximum(i * (tm // HALO) - 1, 0), 0)
    return pl.pallas_call(
        _ffn_kernel,
        grid=(S // tm, nfc),
        in_specs=[pl.BlockSpec((tm, D), row), pl.BlockSpec((HALO, D), halo), pl.BlockSpec((1, D), fix),
                  pl.BlockSpec((D, fc), lo), pl.BlockSpec((D, fc), hi),
                  pl.BlockSpec((CONV_WIDTH, fc), lo), pl.BlockSpec((CONV_WIDTH, fc), hi),
                  pl.BlockSpec((1, fc), lo), pl.BlockSpec((1, fc), hi),
                  pl.BlockSpec((fc, D), lambda i, c: (c, 0)), pl.BlockSpec((1, D), fix)],
        out_specs=pl.BlockSpec((tm, D), row),
        out_shape=jax.ShapeDtypeStruct((S, D), F32),
        scratch_shapes=[pltpu.VMEM((HALO + tm, D), BF16), pltpu.VMEM((tm, D), F32)],
        compiler_params=pltpu.CompilerParams(dimension_semantics=("arbitrary", "arbitrary"),
                                             vmem_limit_bytes=VMEM_LIMIT),
        name="convffn",
    )(h, h, gain, w_up, w_up, conv_w, conv_w, conv_b, conv_b, w_down, pg)


def _ple_kernel(h_ref, p_ref, g_ref, wg_ref, wp_ref, out_ref):
    h = h_ref[...]
    gate = jax.nn.sigmoid(jnp.dot(_rms(h, g_ref[...]).astype(BF16), wg_ref[...], preferred_element_type=F32))
    proj = jnp.dot(p_ref[...].astype(BF16), wp_ref[...], preferred_element_type=F32)
    out_ref[...] = h + gate * proj


def _ple(h, p, gain, wg, wp, tm=512):
    S, D = h.shape
    P = p.shape[1]
    row = lambda i: (i, 0)
    fix = lambda i: (0, 0)
    return pl.pallas_call(
        _ple_kernel,
        grid=(S // tm,),
        in_specs=[pl.BlockSpec((tm, D), row), pl.BlockSpec((tm, P), row), pl.BlockSpec((1, D), fix),
                  pl.BlockSpec((D, D), fix), pl.BlockSpec((P, D), fix)],
        out_specs=pl.BlockSpec((tm, D), row),
        out_shape=jax.ShapeDtypeStruct((S, D), F32),
        compiler_params=pltpu.CompilerParams(dimension_semantics=("arbitrary",), vmem_limit_bytes=VMEM_LIMIT),
        name="ple",
    )(h, p, gain, wg, wp)


def _rope_tables(pos):
    half = ROT_DIM // 2
    inv = ROPE_THETA ** (-jnp.arange(half, dtype=F32) / half)
    ang = pos.astype(F32)[:, None] * inv[None, :]
    cos, sin = jnp.cos(ang), jnp.sin(ang)
    n = pos.shape[0]
    one = jnp.ones((n, HEAD_DIM - ROT_DIM), F32)
    zero = jnp.zeros((n, HEAD_DIM - ROT_DIM), F32)
    zh = jnp.zeros((n, half), F32)
    c = jnp.concatenate([cos, cos, one], axis=1)
    sm = jnp.concatenate([-sin, zh, zero], axis=1)
    sp = jnp.concatenate([zh, sin, zero], axis=1)
    dup = lambda t: jnp.concatenate([t, t], axis=1)
    return dup(c), dup(sm), dup(sp)


def _mask_tables(ncp):
    qo = np.arange(Q_TILE)[None, :]
    rel = (np.arange(2 * ncp) - ncp)[:, None]
    cmp_ok = rel * CMP_STRIDE + CMP_LEN - 1 <= qo
    r = np.arange(WINDOW + Q_TILE)[:, None]
    win_ok = (r - WINDOW <= qo) & (r > qo)
    loc_ok = np.arange(Q_TILE)[:, None] <= qo
    tab = lambda ok: jnp.asarray(np.where(ok, 0.0, NEG), dtype=F32)
    return tab(cmp_ok), tab(win_ok), tab(loc_ok)


def _layer(h, p, prm, tabs):
    S, D = h.shape
    G = N_GROUPS
    nqb = S // Q_TILE
    nch = S // CMP_STRIDE
    rc, rm, rp, crc, crm, crp, cbt, wb, lb, onehot = tabs

    sizes = [ATTN_WIDTH] + [KV_WIDTH] * 6 + [N_HEADS * N_BRANCH, MLP_WIDTH, MLP_WIDTH]
    offs = np.concatenate([[0], np.cumsum(sizes)])
    w_in = prm["w_in"]
    seg = lambda k: w_in[:, offs[k]:offs[k + 1]]
    wg_pad = jnp.pad(seg(7), ((0, 0), (0, LANES - N_HEADS * N_BRANCH)))
    w_cat = jnp.concatenate([seg(0)] + [seg(k) for k in range(1, 7)] + [seg(8), seg(9), wg_pad], axis=1).astype(BF16)

    q, kv, u, v, gates = _inproj(h, prm["pre_mix_g"][None], w_cat, rc, rm, rp,
                                 prm["gmlp_ln_g"][None], prm["gmlp_ln_b"][None])

    craw = jnp.stack([kv[:, 0:KV_WIDTH], kv[:, KV_WIDTH:2 * KV_WIDTH]])
    chunks = craw.reshape(2, S, G, HEAD_DIM).transpose(0, 2, 1, 3).reshape(2, G, nch, CMP_STRIDE * HEAD_DIM)
    w1 = prm["cmp_w1"].astype(BF16)
    half = CMP_STRIDE * HEAD_DIM
    w1c = jnp.concatenate([w1[:, :half], w1[:, half:]], axis=2)
    pe = jnp.broadcast_to(prm["cmp_pe"].reshape(2, 1, CMP_LEN * HEAD_DIM), (2, 8, CMP_LEN * HEAD_DIM)).astype(BF16)
    w2 = jnp.pad(prm["cmp_w2"], ((0, 0), (0, 0), (0, LANES - HEAD_DIM))).astype(BF16)
    b2 = jnp.pad(prm["cmp_b2"], ((0, 0), (0, LANES - HEAD_DIM)))[:, None]
    ident = (jnp.ones_like(crc), jnp.zeros_like(crc), jnp.zeros_like(crc))
    comp = _compress(chunks, w1c, w1, pe, prm["cmp_b1"][:, None], w2, b2,
                     jnp.stack([crc, ident[0]]), jnp.stack([crm, ident[1]]), jnp.stack([crp, ident[2]]))
    kc = comp[0]
    vcT = comp[1][:, :, :HEAD_DIM].transpose(0, 2, 1)

    qT = q.reshape(nqb, Q_TILE, G, GQA, HEAD_DIM).transpose(2, 0, 4, 3, 1).reshape(G, nqb, HEAD_DIM, ROWS)
    gT = (gates[:, :N_HEADS * N_BRANCH].reshape(nqb, Q_TILE, G, GQA, N_BRANCH)
          .transpose(2, 0, 4, 3, 1).reshape(G, nqb, N_BRANCH, ROWS))
    grp = lambda j: kv[:, j * KV_WIDTH:(j + 1) * KV_WIDTH].reshape(S, G, HEAD_DIM).transpose(1, 0, 2)
    ksel, vsel, kwin, vwin = grp(2), grp(3), grp(4), grp(5)
    ksa = jnp.concatenate([ksel, jnp.zeros((G, S, HEAD_DIM), BF16),
                           jnp.broadcast_to(onehot[None], (G, S, BIAS_WIN))], axis=2)
    ones_rows = jnp.zeros((G, BF16_ROWS, S), BF16).at[:, 0, :].set(1.0)
    vsa = jnp.concatenate([vsel.transpose(0, 2, 1), ones_rows], axis=1)
    kwa = jnp.concatenate([kwin, jnp.zeros((G, S, LANES - HEAD_DIM), BF16)], axis=2)
    pad = jnp.zeros((G, WINDOW, LANES), BF16).at[:, :, HEAD_DIM].set(1.0)
    kwa = jnp.concatenate([pad, kwa], axis=1)
    vwa = jnp.concatenate([vwin.transpose(0, 2, 1), ones_rows], axis=1)
    vwa = jnp.pad(vwa, ((0, 0), (0, 0), (WINDOW, 0)))
    oT = _attention(qT, gT, ksa, vsa, kwa, vwa, kc, vcT, cbt, wb, lb)
    attn = oT.reshape(G, nqb, HEAD_DIM, GQA, Q_TILE).transpose(1, 4, 0, 3, 2).reshape(S, ATTN_WIDTH)

    bs_exp = jnp.repeat(prm["gmlp_bs"].T, HEAD_DIM, axis=1)
    mlpn = _gmlp(u, v, prm["gmlp_ws"], bs_exp, prm["mlp_out_g"][None])

    w_o = prm["w_o"].astype(BF16)
    h = _outproj(h, attn, mlpn, prm["attn_out_g"][None], w_o[:ATTN_WIDTH], w_o[ATTN_WIDTH:], prm["post_mix_g"][None])
    h = _ffn(h, prm["pre_ffn_g"][None], prm["w_up"].astype(BF16), prm["conv_w"], prm["conv_b"][None],
             prm["w_down"].astype(BF16), prm["post_ffn_g"][None])
    h = _ple(h, p, prm["ple_norm_g"][None], prm["w_ple_gate"].astype(BF16), prm["w_ple_proj"].astype(BF16))
    return h


def kernel(x, p, pre_mix_g, w_in, cmp_pe, cmp_w1, cmp_b1, cmp_w2, cmp_b2, gmlp_ln_g, gmlp_ln_b, gmlp_ws, gmlp_bs,
           attn_out_g, mlp_out_g, w_o, post_mix_g, pre_ffn_g, w_up, conv_w, conv_b, w_down, post_ffn_g,
           ple_norm_g, w_ple_gate, w_ple_proj):
    B, S, D = x.shape
    depth = p.shape[0]
    assert S % (BIAS_WIN * SEL_LEN) == 0 and D % LANES == 0
    nch = S // CMP_STRIDE
    pos = jnp.arange(S)
    rc, rm, rp = _rope_tables(pos)
    crc, crm, crp = _rope_tables(jnp.arange(nch) * CMP_STRIDE + CMP_LEN - 1)
    onehot = ((pos // SEL_LEN) % BIAS_WIN)[:, None] == jnp.arange(BIAS_WIN)[None, :]
    tabs = (rc, rm, rp, crc, crm, crp) + _mask_tables(nch) + (onehot.astype(BF16),)
    stacked = dict(pre_mix_g=pre_mix_g, w_in=w_in, cmp_pe=cmp_pe, cmp_w1=cmp_w1, cmp_b1=cmp_b1, cmp_w2=cmp_w2,
                   cmp_b2=cmp_b2, gmlp_ln_g=gmlp_ln_g, gmlp_ln_b=gmlp_ln_b, gmlp_ws=gmlp_ws, gmlp_bs=gmlp_bs,
                   attn_out_g=attn_out_g, mlp_out_g=mlp_out_g, w_o=w_o, post_mix_g=post_mix_g, pre_ffn_g=pre_ffn_g,
                   w_up=w_up, conv_w=conv_w, conv_b=conv_b, w_down=w_down, post_ffn_g=post_ffn_g,
                   ple_norm_g=ple_norm_g, w_ple_gate=w_ple_gate, w_ple_proj=w_ple_proj)
    outs = []
    for bi in range(B):
        h = x[bi]
        for i in range(depth):
            h = _layer(h, p[i, bi], {k: v[i] for k, v in stacked.items()}, tabs)
        outs.append(h)
    return jnp.stack(outs)
```

```python
import functools
import math

import numpy as np
import jax
import jax.numpy as jnp
from jax import lax
from jax.experimental import pallas as pl
from jax.experimental.pallas import tpu as pltpu

F32 = jnp.float32
BF16 = jnp.bfloat16

HEAD_DIM = 64
N_HEADS = 8
N_GROUPS = 2
GQA = N_HEADS // N_GROUPS
N_BRANCH = 3
ATTN_WIDTH = N_HEADS * HEAD_DIM
MLP_GROUPS = 8
MLP_WIDTH = MLP_GROUPS * HEAD_DIM
KV_WIDTH = N_GROUPS * HEAD_DIM
ROT_DIM = HEAD_DIM // 4
ROPE_THETA = 500000.0
CMP_LEN = 32
CMP_STRIDE = 16
CMP_HIDDEN = 256
SEL_LEN = 64
N_SELECT = 16
WINDOW = 512
CHUNK = 128
CONV_WIDTH = 3
NORM_EPS = 1e-6
NEG = -1e30
SCALE = HEAD_DIM ** -0.5
QSCALE = SCALE * math.log2(math.e)

LANES = 128
BF16_ROWS = 16
Q_TILE = 128
ROWS = GQA * Q_TILE
KV_TILE = 1024
BIAS_WIN = 128
CMP_PER_Q = Q_TILE // CMP_STRIDE
SEL_PER_Q = Q_TILE // SEL_LEN
CMP_CLASSES = 4
N_FORCED = 3
GATE_SLOT = 16
WIN_EARLY = WINDOW // Q_TILE
IMP_TAPS = range(-1, SEL_LEN // CMP_STRIDE)
VMEM_LIMIT = 56 * 1024 * 1024


def _rms(x, gain):
    return x * lax.rsqrt(jnp.mean(x * x, axis=-1, keepdims=True) + NORM_EPS) * gain


def _rope_slab(x, c, sm, sp):
    return x * c + pltpu.roll(x, LANES - ROT_DIM // 2, 1) * sm + pltpu.roll(x, ROT_DIM // 2, 1) * sp


def _colmax(s):
    return jnp.max(s, axis=0, keepdims=True)


def _tile_heads(x):
    return jnp.concatenate([x] * GQA, axis=1)


def _resident(shape, index_map):
    return pl.BlockSpec(shape, index_map, pipeline_mode=pl.Buffered(1))


def _inproj_kernel(x_ref, g_ref, w_ref, rc_ref, rm_ref, rp_ref, lng_ref, lnb_ref,
                   q_ref, ksa_ref, kwa_ref, vsT_ref, vwT_ref, kcx_ref, vcx_ref, u_ref, v_ref, gate_ref, craw_ref):
    tm = x_ref.shape[0]
    a = _rms(x_ref[...], g_ref[...]).astype(BF16)
    c, sm, sp = rc_ref[...], rm_ref[...], rp_ref[...]
    zq = jnp.dot(a, w_ref[:, 0:ATTN_WIDTH], preferred_element_type=F32)
    for j in range(ATTN_WIDTH // LANES):
        sl = slice(LANES * j, LANES * (j + 1))
        q_ref[:, sl] = (_rope_slab(zq[:, sl], c, sm, sp) * QSCALE).astype(BF16)
    o = ATTN_WIDTH
    zkv = jnp.dot(a, w_ref[:, o:o + 6 * KV_WIDTH], preferred_element_type=F32)
    slab = lambda j: zkv[:, LANES * j:LANES * (j + 1)]

    craw_ref[0] = slab(0)
    craw_ref[1] = slab(1)
    nch = tm // CMP_STRIDE
    for l in range(CMP_STRIDE):
        kcx_ref[l] = craw_ref[0, pl.ds(l, nch, stride=CMP_STRIDE), :].astype(BF16)
        vcx_ref[l] = craw_ref[1, pl.ds(l, nch, stride=CMP_STRIDE), :].astype(BF16)

    pos = pl.program_id(0) * tm + lax.broadcasted_iota(jnp.int32, (tm, BIAS_WIN), 0)
    lane = lax.broadcasted_iota(jnp.int32, (tm, BIAS_WIN), 1)
    ksa_ref[:, 0:LANES] = _rope_slab(slab(2), c, sm, sp).astype(BF16)
    ksa_ref[:, LANES:] = jnp.where((pos // SEL_LEN) % BIAS_WIN == lane, 1.0, 0.0).astype(BF16)
    kwa_ref[...] = _rope_slab(slab(4), c, sm, sp).astype(BF16)
    for k in range(tm // LANES):
        rs = slice(LANES * k, LANES * (k + 1))
        vsT_ref[:, rs] = slab(3)[rs, :].T.astype(BF16)
        vwT_ref[:, rs] = slab(5)[rs, :].T.astype(BF16)

    o += 6 * KV_WIDTH
    zu = jnp.dot(a, w_ref[:, o:o + MLP_WIDTH], preferred_element_type=F32)
    u_ref[...] = jax.nn.gelu(zu).astype(BF16)
    o += MLP_WIDTH
    zv = jax.nn.gelu(jnp.dot(a, w_ref[:, o:o + MLP_WIDTH], preferred_element_type=F32))
    mu = jnp.mean(zv, axis=-1, keepdims=True)
    d = zv - mu
    var = jnp.mean(d * d, axis=-1, keepdims=True)
    v_ref[...] = (d * lax.rsqrt(var + NORM_EPS) * lng_ref[...] + lnb_ref[...]).astype(BF16)
    o += MLP_WIDTH
    zg = jnp.dot(a, w_ref[:, o:o + LANES], preferred_element_type=F32)
    gate_ref[...] = jax.nn.sigmoid(zg)


def _inproj(h, gain, w, rc, rm, rp, lng, lnb, tm=512):
    S, D = h.shape
    N = w.shape[1]
    nch = S // CMP_STRIDE
    row = lambda i: (i, 0)
    col = lambda i: (0, i)
    fix = lambda i: (0, 0)
    chunked = lambda i: (0, i, 0)
    bf = lambda *shape: jax.ShapeDtypeStruct(shape, BF16)
    return pl.pallas_call(
        _inproj_kernel,
        grid=(S // tm,),
        in_specs=[pl.BlockSpec((tm, D), row), _resident((1, D), fix), _resident((D, N), fix),
                  pl.BlockSpec((tm, LANES), row), pl.BlockSpec((tm, LANES), row), pl.BlockSpec((tm, LANES), row),
                  _resident((1, MLP_WIDTH), fix), _resident((1, MLP_WIDTH), fix)],
        out_specs=[pl.BlockSpec((tm, ATTN_WIDTH), row), pl.BlockSpec((tm, 2 * LANES), row),
                   pl.BlockSpec((tm, LANES), row), pl.BlockSpec((LANES, tm), col), pl.BlockSpec((LANES, tm), col),
                   pl.BlockSpec((CMP_STRIDE, tm // CMP_STRIDE, LANES), chunked),
                   pl.BlockSpec((CMP_STRIDE, tm // CMP_STRIDE, LANES), chunked),
                   pl.BlockSpec((tm, MLP_WIDTH), row), pl.BlockSpec((tm, MLP_WIDTH), row),
                   pl.BlockSpec((tm, LANES), row)],
        out_shape=[bf(S, ATTN_WIDTH), bf(S, 2 * LANES), bf(S, LANES), bf(LANES, S), bf(LANES, S),
                   bf(CMP_STRIDE, nch, LANES), bf(CMP_STRIDE, nch, LANES), bf(S, MLP_WIDTH), bf(S, MLP_WIDTH),
                   jax.ShapeDtypeStruct((S, LANES), F32)],
        scratch_shapes=[pltpu.VMEM((2, tm, LANES), F32)],
        compiler_params=pltpu.CompilerParams(dimension_semantics=("arbitrary",), vmem_limit_bytes=VMEM_LIMIT),
        name="inproj",
    )(h, gain, w, rc, rm, rp, lng, lnb)


def _compress_kernel(xk_ref, xv_ref, wb_ref, w1_ref, pe_ref, b1_ref, w2_ref, b2_ref, rc_ref, rm_ref, rp_ref,
                     kc_ref, vcT_ref):
    nch = xk_ref.shape[1]

    def run(x_ref, emit):
        x = jnp.concatenate([x_ref[l] for l in range(CMP_STRIDE)], axis=1)
        c = jnp.dot(x, wb_ref[0], preferred_element_type=F32)
        cvec = jnp.dot(pe_ref[0], w1_ref[0], preferred_element_type=F32)[0:1] + b1_ref[0]
        for g in range(N_GROUPS):
            cg = c[:, 2 * CMP_HIDDEN * g:2 * CMP_HIDDEN * (g + 1)]
            h = jax.nn.gelu(cg[:, :CMP_HIDDEN] + pltpu.roll(cg[:, CMP_HIDDEN:], nch - 1, 0) + cvec)
            emit(g, jnp.dot(h.astype(BF16), w2_ref[0], preferred_element_type=F32) + b2_ref[0])

    def emit_k(g, o):
        kc_ref[g] = _rope_slab(o, rc_ref[...], rm_ref[...], rp_ref[...]).astype(BF16)

    def emit_v(g, o):
        for k in range(nch // LANES):
            rs = slice(LANES * k, LANES * (k + 1))
            vcT_ref[g, :, rs] = o[rs, :].T[0:HEAD_DIM].astype(BF16)

    @pl.when(pl.program_id(0) == 0)
    def _():
        run(xk_ref, emit_k)

    @pl.when(pl.program_id(0) == 1)
    def _():
        run(xv_ref, emit_v)


def _compress(kcx, vcx, wboth, w1, pe, b1, w2, b2, rc, rm, rp):
    _, nch, _ = kcx.shape
    cw = CMP_STRIDE * HEAD_DIM
    kv3 = lambda k: (k, 0, 0)
    fix3 = lambda k: (0, 0, 0)
    fix2 = lambda k: (0, 0)
    return pl.pallas_call(
        _compress_kernel,
        grid=(2,),
        in_specs=[_resident((CMP_STRIDE, nch, LANES), fix3), _resident((CMP_STRIDE, nch, LANES), fix3),
                  pl.BlockSpec((1, CMP_STRIDE * LANES, 2 * N_GROUPS * CMP_HIDDEN), kv3),
                  pl.BlockSpec((1, 2 * cw, CMP_HIDDEN), kv3), pl.BlockSpec((1, 8, 2 * cw), kv3),
                  pl.BlockSpec((1, 1, CMP_HIDDEN), kv3), pl.BlockSpec((1, CMP_HIDDEN, LANES), kv3),
                  pl.BlockSpec((1, 1, LANES), kv3),
                  _resident((nch, LANES), fix2), _resident((nch, LANES), fix2), _resident((nch, LANES), fix2)],
        out_specs=[pl.BlockSpec((N_GROUPS, nch, LANES), fix3), pl.BlockSpec((N_GROUPS, HEAD_DIM, nch), fix3)],
        out_shape=[jax.ShapeDtypeStruct((N_GROUPS, nch, LANES), BF16),
                   jax.ShapeDtypeStruct((N_GROUPS, HEAD_DIM, nch), BF16)],
        compiler_params=pltpu.CompilerParams(dimension_semantics=("arbitrary",), vmem_limit_bytes=VMEM_LIMIT),
        name="compress",
    )(kcx, vcx, wboth, w1, pe, b1, w2, b2, rc, rm, rp)


def _attn_kernel(q_ref, gate_ref, ksa_ref, vsT_ref, kwa_ref, vwT_ref, kc_ref, vcT_ref, cbt_ref, wb_ref, lb_ref,
                 ones_ref, out_ref, qaug_ref, sbuf_ref, acc_ref, l_ref, ma_ref, ms_ref, ps_ref, oc_ref, sel_ref,
                 *, ns, ncp):
    g = pl.program_id(0)
    b = pl.program_id(1)
    t0 = pl.multiple_of(b * Q_TILE, Q_TILE)

    qf = q_ref[...].astype(F32)
    parts = []
    for j in range(GQA // 2):
        t = qf[:, LANES * j:LANES * (j + 1)].T
        parts += [t[0:HEAD_DIM], t[HEAD_DIM:]]
    qT = jnp.concatenate(parts, axis=1).astype(BF16)
    zeros = jnp.zeros((HEAD_DIM, ROWS), BF16)
    q0T = jnp.concatenate([qT, zeros], axis=0)
    qlo = jnp.where(g == 0, qT, zeros)
    qhi = jnp.where(g == 0, zeros, qT)
    qgT = jnp.concatenate([qlo, qhi], axis=0)

    ps_ref[0:8, :] = jnp.zeros((8, Q_TILE), F32)
    step = ncp // CMP_CLASSES
    cls = (b * CMP_PER_Q + CMP_PER_Q - 2) // step
    for k in range(CMP_CLASSES):
        @pl.when(cls == k)
        def _(nk=(k + 1) * step):
            cb = cbt_ref[pl.ds(pl.multiple_of(ncp - b * CMP_PER_Q, 8), nk), :]
            s = jnp.dot(kc_ref[0, 0:nk, :], q0T, preferred_element_type=F32) + _tile_heads(cb)
            m = _colmax(s)
            e = jnp.exp2(s - m)
            inv = jnp.where(m > NEG * 0.5, 1.0 / jnp.maximum(jnp.sum(e, axis=0, keepdims=True), 1e-30), 0.0)
            p = e * inv
            oc_ref[...] = jnp.dot(vcT_ref[0, :, 0:nk], p.astype(BF16), preferred_element_type=F32)
            ps = p[:, 0:Q_TILE]
            for r in range(1, GQA):
                ps = ps + p[:, r * Q_TILE:(r + 1) * Q_TILE]
            ps_ref[8:8 + nk, :] = ps
            if nk < ncp:
                ps_ref[8 + nk:, :] = jnp.zeros((ncp - nk, Q_TILE), F32)

    ratio = SEL_LEN // CMP_STRIDE
    imp = sum(ps_ref[pl.ds(8 + d, ns, stride=ratio), :] for d in IMP_TAPS)

    blk = lax.broadcasted_iota(jnp.int32, (ns, Q_TILE), 0)
    cur = (t0 + lax.broadcasted_iota(jnp.int32, (ns, Q_TILE), 1)) // SEL_LEN
    valid = blk <= cur
    forced = valid & ((blk == 0) | (blk == cur) | (blk == cur - 1))
    rank0 = jnp.where(valid & jnp.logical_not(forced), imp, NEG)
    blk_f = blk.astype(F32)
    taken = -(2.0 ** 126)

    def rank(break_ties):
        score = rank0
        for _ in range(N_SELECT - N_FORCED):
            mx = _colmax(score)
            if break_ties:
                idx = jnp.min(jnp.where(score == mx, blk_f, float(ns)), axis=0, keepdims=True)
                hit = blk_f == idx
            else:
                hit = score == mx
            score = jnp.where(hit, taken, score)
        return (score == taken) & (rank0 > NEG * 0.5)

    picked = rank(False)
    count = jnp.sum(jnp.where(picked, 1.0, 0.0), axis=0, keepdims=True)
    want = jnp.clip(cur[0:1] - (N_FORCED - 1), 0, N_SELECT - N_FORCED).astype(F32)
    sel_ref[...] = jnp.where(forced | picked, 1.0, 0.0)

    @pl.when(jnp.logical_not(jnp.all(count == want)))
    def _():
        sel_ref[...] = jnp.where(forced | rank(True), 1.0, 0.0)

    sel = sel_ref[...] > 0.5
    bias = jnp.where(sel & (blk < b * SEL_PER_Q), 0.0, NEG).astype(BF16)
    for w in range(ns // BIAS_WIN):
        qaug_ref[w, 0:2 * HEAD_DIM, :] = qgT
        qaug_ref[w, 2 * HEAD_DIM:, :] = _tile_heads(bias[BIAS_WIN * w:BIAS_WIN * (w + 1), :])

    ones = ones_ref[...]
    s = (jnp.dot(ksa_ref[pl.ds(t0, Q_TILE), 0:LANES], qgT, preferred_element_type=F32)
         + _tile_heads(lb_ref[...]))
    m0 = _colmax(s)
    p0 = jnp.exp2(s - m0).astype(BF16)
    acc_ref[...] = jnp.dot(vsT_ref[:, pl.ds(t0, Q_TILE)], p0, preferred_element_type=F32)
    l_ref[...] = jnp.dot(ones[:, 0:Q_TILE], p0, preferred_element_type=F32)
    ma_ref[...] = m0

    n_tiles = jnp.maximum((t0 + KV_TILE - 1) // KV_TILE, 1)
    tiles_per_win = BIAS_WIN * SEL_LEN // KV_TILE

    def scores(t, slot):
        start = pl.multiple_of(t * KV_TILE, KV_TILE)
        s = jnp.dot(ksa_ref[pl.ds(start, KV_TILE), :], qaug_ref[t // tiles_per_win],
                    preferred_element_type=F32)
        sbuf_ref[slot] = s
        ms_ref[...] = jnp.maximum(ms_ref[...], _colmax(s))

    def consume(t, slot, m_s):
        start = pl.multiple_of(t * KV_TILE, KV_TILE)
        alpha = jnp.exp2(ma_ref[...] - m_s)
        pp = jnp.exp2(sbuf_ref[slot] - m_s).astype(BF16)
        acc_ref[...] = alpha * acc_ref[...] + jnp.dot(vsT_ref[:, pl.ds(start, KV_TILE)], pp,
                                                      preferred_element_type=F32)
        l_ref[...] = alpha * l_ref[...] + jnp.dot(ones, pp, preferred_element_type=F32)
        ma_ref[...] = m_s

    def step_(t, slot):
        m_s = ms_ref[...]
        scores(t + 1, 1 - slot)
        consume(t, slot, m_s)

    ms_ref[...] = m0
    scores(0, 0)

    def body(i, carry):
        step_(2 * i, 0)
        step_(2 * i + 1, 1)
        return carry

    n_steps = n_tiles - 1
    lax.fori_loop(0, n_steps // 2, body, 0)

    @pl.when(n_steps % 2 == 1)
    def _():
        step_(n_steps - 1, 0)

    consume(n_steps, n_steps % 2, ms_ref[...])

    wk = WINDOW + Q_TILE
    ws = pl.multiple_of(jnp.maximum(t0 - WINDOW, 0), Q_TILE)
    wb = wb_ref[jnp.minimum(b, WIN_EARLY)]
    s = jnp.dot(kwa_ref[pl.ds(ws, wk), :], qgT, preferred_element_type=F32) + _tile_heads(wb)
    pw = jnp.exp2(s - _colmax(s)).astype(BF16)
    ow = jnp.dot(vwT_ref[:, pl.ds(ws, wk)], pw, preferred_element_type=F32)
    lw = jnp.dot(ones[:, 0:wk], pw, preferred_element_type=F32)[0:1]
    owT = ow / jnp.maximum(lw, 1e-30)
    osT = acc_ref[...] / jnp.maximum(l_ref[0:1], 1e-30)

    gall = gate_ref[...].T
    gg = jnp.where(g == 0, gall[0:GATE_SLOT], gall[GATE_SLOT:2 * GATE_SLOT])
    gt = lambda br: jnp.concatenate([gg[br * GQA + r:br * GQA + r + 1] for r in range(GQA)], axis=1)
    oT = gt(0) * oc_ref[...] + gt(1) * osT + gt(2) * owT
    for j in range(GQA // 2):
        pair = jnp.concatenate([oT[:, Q_TILE * 2 * j:Q_TILE * (2 * j + 1)],
                                oT[:, Q_TILE * (2 * j + 1):Q_TILE * (2 * j + 2)]], axis=0)
        out_ref[:, LANES * j:LANES * (j + 1)] = pair.T


def _attention(q, gates, ksa, vsT, kwa, vwT, kc, vcT, cbt, wb, lb, ones):
    S = q.shape[0]
    nqb = S // Q_TILE
    ncp = kc.shape[1]
    ns = S // SEL_LEN
    gw = GQA * HEAD_DIM
    fix = lambda g, b: (0, 0)
    per_g3 = lambda g, b: (g, 0, 0)
    kernel = functools.partial(_attn_kernel, ns=ns, ncp=ncp)
    return pl.pallas_call(
        kernel,
        grid=(N_GROUPS, nqb),
        in_specs=[pl.BlockSpec((Q_TILE, gw), lambda g, b: (b, g)), pl.BlockSpec((Q_TILE, LANES), lambda g, b: (b, 0)),
                  _resident((S, 2 * LANES), fix), pl.BlockSpec((HEAD_DIM, S), lambda g, b: (g, 0)),
                  _resident((S, LANES), fix), pl.BlockSpec((HEAD_DIM, S), lambda g, b: (g, 0)),
                  pl.BlockSpec((1, ncp, LANES), per_g3), pl.BlockSpec((1, HEAD_DIM, ncp), per_g3),
                  _resident((2 * ncp, Q_TILE), fix),
                  _resident((WIN_EARLY + 1, WINDOW + Q_TILE, Q_TILE), lambda g, b: (0, 0, 0)),
                  _resident((Q_TILE, Q_TILE), fix), _resident((BF16_ROWS, KV_TILE), fix)],
        out_specs=pl.BlockSpec((Q_TILE, gw), lambda g, b: (b, g)),
        out_shape=jax.ShapeDtypeStruct((S, ATTN_WIDTH), F32),
        scratch_shapes=[pltpu.VMEM((ns // BIAS_WIN, 2 * LANES, ROWS), BF16),
                        pltpu.VMEM((2, KV_TILE, ROWS), F32),
                        pltpu.VMEM((HEAD_DIM, ROWS), F32),
                        pltpu.VMEM((BF16_ROWS, ROWS), F32),
                        pltpu.VMEM((1, ROWS), F32), pltpu.VMEM((1, ROWS), F32),
                        pltpu.VMEM((8 + ncp, Q_TILE), F32),
                        pltpu.VMEM((HEAD_DIM, ROWS), F32),
                        pltpu.VMEM((ns, Q_TILE), F32)],
        compiler_params=pltpu.CompilerParams(dimension_semantics=("arbitrary", "arbitrary"),
                                             vmem_limit_bytes=VMEM_LIMIT),
        name="nsa_attention",
    )(q, gates, ksa, vsT, kwa, vwT, kc, vcT, cbt, wb, lb, ones)


def _gmlp_kernel(u_ref, v_ref, ws_ref, bs_ref, g_ref, out_ref, wsm_ref, *, chunks):
    @pl.when(pl.program_id(0) == 0)
    def _():
        r = lax.broadcasted_iota(jnp.int32, (CHUNK, CHUNK), 0)
        c = lax.broadcasted_iota(jnp.int32, (CHUNK, CHUNK), 1)
        for g in range(MLP_GROUPS):
            wsm_ref[g] = jnp.where(c <= r, ws_ref[g], 0.0).astype(BF16)

    lane = lax.broadcasted_iota(jnp.int32, (CHUNK, LANES), 1)
    left = lane < HEAD_DIM
    for ci in range(chunks):
        rows = slice(ci * CHUNK, (ci + 1) * CHUNK)
        parts = []
        for pr in range(MLP_GROUPS // 2):
            vp = v_ref[rows, pr * LANES:(pr + 1) * LANES]
            va = jnp.where(left, vp, jnp.zeros_like(vp))
            vb = jnp.where(left, jnp.zeros_like(vp), vp)
            parts.append(jnp.dot(wsm_ref[2 * pr], va, preferred_element_type=F32)
                         + jnp.dot(wsm_ref[2 * pr + 1], vb, preferred_element_type=F32))
        mixed = jnp.concatenate(parts, axis=1) + bs_ref[...]
        y = u_ref[rows, :].astype(F32) * mixed
        out_ref[rows, :] = _rms(y, g_ref[...]).astype(BF16)


def _gmlp(u, v, ws, bs_exp, gain, chunks=4):
    S = u.shape[0]
    tm = CHUNK * chunks
    row = lambda i: (i, 0)
    return pl.pallas_call(
        functools.partial(_gmlp_kernel, chunks=chunks),
        grid=(S // tm,),
        in_specs=[pl.BlockSpec((tm, MLP_WIDTH), row), pl.BlockSpec((tm, MLP_WIDTH), row),
                  _resident((MLP_GROUPS, CHUNK, CHUNK), lambda i: (0, 0, 0)),
                  _resident((CHUNK, MLP_WIDTH), lambda i: (0, 0)),
                  _resident((1, MLP_WIDTH), lambda i: (0, 0))],
        out_specs=pl.BlockSpec((tm, MLP_WIDTH), row),
        out_shape=jax.ShapeDtypeStruct((S, MLP_WIDTH), BF16),
        scratch_shapes=[pltpu.VMEM((MLP_GROUPS, CHUNK, CHUNK), BF16)],
        compiler_params=pltpu.CompilerParams(dimension_semantics=("arbitrary",), vmem_limit_bytes=VMEM_LIMIT),
        name="gmlp",
    )(u, v, ws, bs_exp, gain)


def _outproj_kernel(h_ref, a_ref, m_ref, ag_ref, wa_ref, wm_ref, pg_ref, out_ref):
    an = _rms(a_ref[...], ag_ref[...]).astype(BF16)
    mix = (jnp.dot(an, wa_ref[...], preferred_element_type=F32)
           + jnp.dot(m_ref[...], wm_ref[...], preferred_element_type=F32))
    out_ref[...] = h_ref[...] + _rms(mix, pg_ref[...])


def _outproj(h, attn, mlpn, ag, wa, wm, pg, tm=512):
    S, D = h.shape
    row = lambda i: (i, 0)
    fix = lambda i: (0, 0)
    return pl.pallas_call(
        _outproj_kernel,
        grid=(S // tm,),
        in_specs=[pl.BlockSpec((tm, D), row), pl.BlockSpec((tm, ATTN_WIDTH), row), pl.BlockSpec((tm, MLP_WIDTH), row),
                  _resident((1, ATTN_WIDTH), fix), _resident((ATTN_WIDTH, D), fix),
                  _resident((MLP_WIDTH, D), fix), _resident((1, D), fix)],
        out_specs=pl.BlockSpec((tm, D), row),
        out_shape=jax.ShapeDtypeStruct((S, D), F32),
        compiler_params=pltpu.CompilerParams(dimension_semantics=("arbitrary",), vmem_limit_bytes=VMEM_LIMIT),
        name="outproj",
    )(h, attn, mlpn, ag, wa, wm, pg)


HALO = BF16_ROWS


def _ffn_kernel(h_ref, halo_ref, p_ref, g_ref, wup_ref, cw_ref, cb_ref, wd_ref, pg_ref, plg_ref, wpg_ref, wpp_ref,
                out_ref, acc_ref, *, fc):
    dff = wd_ref.shape[0]
    h = h_ref[...]
    hn = _rms(halo_ref[...], g_ref[...])
    hn = jnp.where(pl.program_id(0) == 0, 0.0, hn)
    xn = jnp.concatenate([hn.astype(BF16), _rms(h, g_ref[...]).astype(BF16)], axis=0)

    def conv(lo):
        hh = jnp.dot(xn, wup_ref[:, lo:lo + fc], preferred_element_type=F32)
        cw = cw_ref[:, lo:lo + fc]
        y = cb_ref[:, lo:lo + fc] + pltpu.roll(hh, 2, 0) * cw[0:1] + pltpu.roll(hh, 1, 0) * cw[1:2] + hh * cw[2:3]
        return y[HALO:, :]

    for c0 in range(0, dff, fc):
        act = (jax.nn.silu(conv(c0)) * conv(dff + c0)).astype(BF16)
        part = jnp.dot(act, wd_ref[c0:c0 + fc, :], preferred_element_type=F32)
        if c0 == 0:
            acc_ref[...] = part
        else:
            acc_ref[...] += part

    h2 = h + _rms(acc_ref[...], pg_ref[...])
    gate = jax.nn.sigmoid(jnp.dot(_rms(h2, plg_ref[...]).astype(BF16), wpg_ref[...], preferred_element_type=F32))
    proj = jnp.dot(p_ref[...].astype(BF16), wpp_ref[...], preferred_element_type=F32)
    out_ref[...] = h2 + gate * proj


def _ffn(h, p, gain, w_up, conv_w, conv_b, w_down, pg, plg, wpg, wpp, tm=512, fc=256):
    S, D = h.shape
    dff = w_down.shape[0]
    P = p.shape[1]
    row = lambda i: (i, 0)
    fix = lambda i: (0, 0)
    halo = lambda i: (jnp.maximum(i * (tm // HALO) - 1, 0), 0)
    return pl.pallas_call(
        functools.partial(_ffn_kernel, fc=fc),
        grid=(S // tm,),
        in_specs=[pl.BlockSpec((tm, D), row), pl.BlockSpec((HALO, D), halo), pl.BlockSpec((tm, P), row),
                  _resident((1, D), fix), _resident((D, 2 * dff), fix), _resident((CONV_WIDTH, 2 * dff), fix),
                  _resident((1, 2 * dff), fix), _resident((dff, D), fix), _resident((1, D), fix),
                  _resident((1, D), fix), _resident((D, D), fix), _resident((P, D), fix)],
        out_specs=pl.BlockSpec((tm, D), row),
        out_shape=jax.ShapeDtypeStruct((S, D), F32),
        scratch_shapes=[pltpu.VMEM((tm, D), F32)],
        compiler_params=pltpu.CompilerParams(dimension_semantics=("arbitrary",), vmem_limit_bytes=VMEM_LIMIT),
        name="convffn_ple",
    )(h, h, p, gain, w_up, conv_w, conv_b, w_down, pg, plg, wpg, wpp)


def _rope_tables(pos):
    half = ROT_DIM // 2
    inv = ROPE_THETA ** (-np.arange(half, dtype=np.float64) / half)
    ang = np.asarray(pos, np.float64)[:, None] * inv[None, :]
    cos, sin = jnp.asarray(np.cos(ang), dtype=F32), jnp.asarray(np.sin(ang), dtype=F32)
    n = ang.shape[0]
    one = jnp.ones((n, HEAD_DIM - ROT_DIM), F32)
    zero = jnp.zeros((n, HEAD_DIM - ROT_DIM), F32)
    zh = jnp.zeros((n, half), F32)
    c = jnp.concatenate([cos, cos, one], axis=1)
    sm = jnp.concatenate([-sin, zh, zero], axis=1)
    sp = jnp.concatenate([zh, sin, zero], axis=1)
    dup = lambda t: jnp.concatenate([t, t], axis=1)
    return dup(c), dup(sm), dup(sp)


def _mask_tables(ncp):
    qo = np.arange(Q_TILE)[None, :]
    rel = (np.arange(2 * ncp) - ncp)[:, None]
    cmp_ok = rel * CMP_STRIDE + CMP_LEN - 1 <= qo
    r = np.arange(WINDOW + Q_TILE)[:, None]
    win_ok = []
    for v in range(WIN_EARLY):
        tq = v * Q_TILE + qo
        win_ok.append((r <= tq) & (r > tq - WINDOW))
    win_ok.append((r - WINDOW <= qo) & (r > qo))
    loc_ok = np.arange(Q_TILE)[:, None] <= qo
    tab = lambda ok: jnp.asarray(np.where(ok, 0.0, NEG), dtype=F32)
    return tab(cmp_ok), tab(np.stack(win_ok)), tab(loc_ok)


def _gate_columns():
    src = np.full((LANES,), -1)
    for g in range(N_GROUPS):
        for br in range(N_BRANCH):
            for r in range(GQA):
                src[g * GATE_SLOT + br * GQA + r] = (g * GQA + r) * N_BRANCH + br
    return src


def _layer(h, p, prm, tabs):
    S, D = h.shape
    rc, rm, rp, crc, crm, crp, cbt, wb, lb, ones = tabs

    sizes = [ATTN_WIDTH] + [KV_WIDTH] * 6 + [N_HEADS * N_BRANCH, MLP_WIDTH, MLP_WIDTH]
    offs = np.concatenate([[0], np.cumsum(sizes)])
    w_in = prm["w_in"]
    seg = lambda k: w_in[:, offs[k]:offs[k + 1]]
    src = _gate_columns()
    wg = jnp.where(jnp.asarray(src >= 0)[None, :], seg(7)[:, np.maximum(src, 0)], 0.0)
    w_cat = jnp.concatenate([seg(0)] + [seg(k) for k in range(1, 7)] + [seg(8), seg(9), wg], axis=1).astype(BF16)

    q, ksa, kwa, vsT, vwT, kcx, vcx, u, v, gates = _inproj(h, prm["pre_mix_g"][None], w_cat, rc, rm, rp,
                                                           prm["gmlp_ln_g"][None], prm["gmlp_ln_b"][None])

    w1 = prm["cmp_w1"].astype(BF16)
    half = CMP_STRIDE * HEAD_DIM
    w1l = jnp.concatenate([w1[:, :half], w1[:, half:]], axis=2).reshape(2, CMP_STRIDE, HEAD_DIM, 2 * CMP_HIDDEN)
    zl = jnp.zeros_like(w1l)
    wboth = jnp.concatenate([jnp.concatenate([w1l, zl], axis=3), jnp.concatenate([zl, w1l], axis=3)], axis=2)
    wboth = wboth.reshape(2, CMP_STRIDE * LANES, 2 * N_GROUPS * CMP_HIDDEN)
    pe = jnp.broadcast_to(prm["cmp_pe"].reshape(2, 1, CMP_LEN * HEAD_DIM), (2, 8, CMP_LEN * HEAD_DIM)).astype(BF16)
    w2 = jnp.pad(prm["cmp_w2"], ((0, 0), (0, 0), (0, LANES - HEAD_DIM))).astype(BF16)
    b2 = jnp.pad(prm["cmp_b2"], ((0, 0), (0, LANES - HEAD_DIM)))[:, None]
    kc, vcT = _compress(kcx, vcx, wboth, w1, pe, prm["cmp_b1"][:, None], w2, b2, crc, crm, crp)

    attn = _attention(q, gates, ksa, vsT, kwa, vwT, kc, vcT, cbt, wb, lb, ones)

    bs_exp = jnp.repeat(prm["gmlp_bs"].T, HEAD_DIM, axis=1)
    mlpn = _gmlp(u, v, prm["gmlp_ws"], bs_exp, prm["mlp_out_g"][None])

    w_o = prm["w_o"].astype(BF16)
    h = _outproj(h, attn, mlpn, prm["attn_out_g"][None], w_o[:ATTN_WIDTH], w_o[ATTN_WIDTH:], prm["post_mix_g"][None])
    h = _ffn(h, p, prm["pre_ffn_g"][None], prm["w_up"].astype(BF16), prm["conv_w"], prm["conv_b"][None],
             prm["w_down"].astype(BF16), prm["post_ffn_g"][None], prm["ple_norm_g"][None],
             prm["w_ple_gate"].astype(BF16), prm["w_ple_proj"].astype(BF16))
    return h


def kernel(x, p, pre_mix_g, w_in, cmp_pe, cmp_w1, cmp_b1, cmp_w2, cmp_b2, gmlp_ln_g, gmlp_ln_b, gmlp_ws, gmlp_bs,
           attn_out_g, mlp_out_g, w_o, post_mix_g, pre_ffn_g, w_up, conv_w, conv_b, w_down, post_ffn_g,
           ple_norm_g, w_ple_gate, w_ple_proj):
    B, S, D = x.shape
    depth = p.shape[0]
    assert S % (BIAS_WIN * SEL_LEN) == 0 and D % LANES == 0
    nch = S // CMP_STRIDE
    rc, rm, rp = _rope_tables(np.arange(S))
    crc, crm, crp = _rope_tables(np.arange(nch) * CMP_STRIDE + CMP_LEN - 1)
    ones = jnp.ones((BF16_ROWS, KV_TILE), BF16)
    tabs = (rc, rm, rp, crc, crm, crp) + _mask_tables(nch) + (ones,)
    stacked = dict(pre_mix_g=pre_mix_g, w_in=w_in, cmp_pe=cmp_pe, cmp_w1=cmp_w1, cmp_b1=cmp_b1, cmp_w2=cmp_w2,
                   cmp_b2=cmp_b2, gmlp_ln_g=gmlp_ln_g, gmlp_ln_b=gmlp_ln_b, gmlp_ws=gmlp_ws, gmlp_bs=gmlp_bs,
                   attn_out_g=attn_out_g, mlp_out_g=mlp_out_g, w_o=w_o, post_mix_g=post_mix_g, pre_ffn_g=pre_ffn_g,
                   w_up=w_up, conv_w=conv_w, conv_b=conv_b, w_down=w_down, post_ffn_g=post_ffn_g,
                   ple_norm_g=ple_norm_g, w_ple_gate=w_ple_gate, w_ple_proj=w_ple_proj)
    outs = []
    for bi in range(B):
        h = x[bi]
        for i in range(depth):
            h = _layer(h, p[i, bi], {k: v[i] for k, v in stacked.items()}, tabs)
        outs.append(h)
    return jnp.stack(outs)
```

```python
import functools
import math

import numpy as np
import jax
import jax.numpy as jnp
from jax import lax
from jax.experimental import pallas as pl
from jax.experimental.pallas import tpu as pltpu

F32 = jnp.float32
BF16 = jnp.bfloat16

HEAD_DIM = 64
N_HEADS = 8
N_GROUPS = 2
GQA = N_HEADS // N_GROUPS
N_BRANCH = 3
ATTN_WIDTH = N_HEADS * HEAD_DIM
MLP_GROUPS = 8
MLP_WIDTH = MLP_GROUPS * HEAD_DIM
KV_WIDTH = N_GROUPS * HEAD_DIM
ROT_DIM = HEAD_DIM // 4
ROPE_THETA = 500000.0
CMP_LEN = 32
CMP_STRIDE = 16
CMP_HIDDEN = 256
SEL_LEN = 64
N_SELECT = 16
WINDOW = 512
CHUNK = 128
CONV_WIDTH = 3
NORM_EPS = 1e-6
NEG = -1e30
SCALE = HEAD_DIM ** -0.5
QSCALE = SCALE * math.log2(math.e)

LANES = 128
BF16_ROWS = 16
Q_TILE = 128
ROWS = GQA * Q_TILE
KV_TILE = 1024
BIAS_WIN = 128
V_ROWS = HEAD_DIM + BF16_ROWS
CMP_PER_Q = Q_TILE // CMP_STRIDE
SEL_PER_Q = Q_TILE // SEL_LEN
CMP_CLASSES = 4
N_FORCED = 3
GATE_SLOT = 16
WIN_EARLY = WINDOW // Q_TILE
IMP_TAPS = range(-1, SEL_LEN // CMP_STRIDE)
VMEM_LIMIT = 56 * 1024 * 1024


def _rms(x, gain):
    return x * lax.rsqrt(jnp.mean(x * x, axis=-1, keepdims=True) + NORM_EPS) * gain


def _rope_slab(x, c, sm, sp):
    return x * c + pltpu.roll(x, LANES - ROT_DIM // 2, 1) * sm + pltpu.roll(x, ROT_DIM // 2, 1) * sp


def _colmax(s):
    return jnp.max(s, axis=0, keepdims=True)


def _tile_heads(x):
    return jnp.concatenate([x] * GQA, axis=1)


def _resident(shape, index_map):
    return pl.BlockSpec(shape, index_map, pipeline_mode=pl.Buffered(1))


def _inproj_kernel(x_ref, g_ref, w_ref, rc_ref, rm_ref, rp_ref, lng_ref, lnb_ref,
                   q_ref, ksa_ref, kwa_ref, vsT_ref, vwT_ref, kcx_ref, vcx_ref, u_ref, v_ref, gate_ref, craw_ref):
    tm = x_ref.shape[0]
    a = _rms(x_ref[...], g_ref[...]).astype(BF16)
    c, sm, sp = rc_ref[...], rm_ref[...], rp_ref[...]
    zq = jnp.dot(a, w_ref[:, 0:ATTN_WIDTH], preferred_element_type=F32)
    for j in range(ATTN_WIDTH // LANES):
        sl = slice(LANES * j, LANES * (j + 1))
        q_ref[:, sl] = (_rope_slab(zq[:, sl], c, sm, sp) * QSCALE).astype(BF16)
    o = ATTN_WIDTH
    zkv = jnp.dot(a, w_ref[:, o:o + 6 * KV_WIDTH], preferred_element_type=F32)
    slab = lambda j: zkv[:, LANES * j:LANES * (j + 1)]

    craw_ref[0] = slab(0)
    craw_ref[1] = slab(1)
    nch = tm // CMP_STRIDE
    for l in range(CMP_STRIDE):
        kcx_ref[l] = craw_ref[0, pl.ds(l, nch, stride=CMP_STRIDE), :].astype(BF16)
        vcx_ref[l] = craw_ref[1, pl.ds(l, nch, stride=CMP_STRIDE), :].astype(BF16)

    pos = pl.program_id(0) * tm + lax.broadcasted_iota(jnp.int32, (tm, BIAS_WIN), 0)
    lane = lax.broadcasted_iota(jnp.int32, (tm, BIAS_WIN), 1)
    ksa_ref[:, 0:LANES] = _rope_slab(slab(2), c, sm, sp).astype(BF16)
    ksa_ref[:, LANES:] = jnp.where((pos // SEL_LEN) % BIAS_WIN == lane, 1.0, 0.0).astype(BF16)
    kwa_ref[...] = _rope_slab(slab(4), c, sm, sp).astype(BF16)
    for k in range(tm // LANES):
        rs = slice(LANES * k, LANES * (k + 1))
        for ref, sl in ((vsT_ref, slab(3)), (vwT_ref, slab(5))):
            t = sl[rs, :].T.astype(BF16)
            for g in range(N_GROUPS):
                ref[g, 0:HEAD_DIM, rs] = t[HEAD_DIM * g:HEAD_DIM * (g + 1)]
                ref[g, HEAD_DIM:, rs] = jnp.ones((BF16_ROWS, LANES), BF16)

    o += 6 * KV_WIDTH
    zu = jnp.dot(a, w_ref[:, o:o + MLP_WIDTH], preferred_element_type=F32)
    u_ref[...] = jax.nn.gelu(zu).astype(BF16)
    o += MLP_WIDTH
    zv = jax.nn.gelu(jnp.dot(a, w_ref[:, o:o + MLP_WIDTH], preferred_element_type=F32))
    mu = jnp.mean(zv, axis=-1, keepdims=True)
    d = zv - mu
    var = jnp.mean(d * d, axis=-1, keepdims=True)
    v_ref[...] = (d * lax.rsqrt(var + NORM_EPS) * lng_ref[...] + lnb_ref[...]).astype(BF16)
    o += MLP_WIDTH
    zg = jnp.dot(a, w_ref[:, o:o + LANES], preferred_element_type=F32)
    gate_ref[...] = jax.nn.sigmoid(zg)


def _inproj(h, gain, w, rc, rm, rp, lng, lnb, tm=512):
    S, D = h.shape
    N = w.shape[1]
    nch = S // CMP_STRIDE
    row = lambda i: (i, 0)
    fix = lambda i: (0, 0)
    chunked = lambda i: (0, i, 0)
    bf = lambda *shape: jax.ShapeDtypeStruct(shape, BF16)
    return pl.pallas_call(
        _inproj_kernel,
        grid=(S // tm,),
        in_specs=[pl.BlockSpec((tm, D), row), _resident((1, D), fix), _resident((D, N), fix),
                  pl.BlockSpec((tm, LANES), row), pl.BlockSpec((tm, LANES), row), pl.BlockSpec((tm, LANES), row),
                  _resident((1, MLP_WIDTH), fix), _resident((1, MLP_WIDTH), fix)],
        out_specs=[pl.BlockSpec((tm, ATTN_WIDTH), row), pl.BlockSpec((tm, 2 * LANES), row),
                   pl.BlockSpec((tm, LANES), row), pl.BlockSpec((N_GROUPS, V_ROWS, tm), lambda i: (0, 0, i)),
                   pl.BlockSpec((N_GROUPS, V_ROWS, tm), lambda i: (0, 0, i)),
                   pl.BlockSpec((CMP_STRIDE, tm // CMP_STRIDE, LANES), chunked),
                   pl.BlockSpec((CMP_STRIDE, tm // CMP_STRIDE, LANES), chunked),
                   pl.BlockSpec((tm, MLP_WIDTH), row), pl.BlockSpec((tm, MLP_WIDTH), row),
                   pl.BlockSpec((tm, LANES), row)],
        out_shape=[bf(S, ATTN_WIDTH), bf(S, 2 * LANES), bf(S, LANES), bf(N_GROUPS, V_ROWS, S), bf(N_GROUPS, V_ROWS, S),
                   bf(CMP_STRIDE, nch, LANES), bf(CMP_STRIDE, nch, LANES), bf(S, MLP_WIDTH), bf(S, MLP_WIDTH),
                   jax.ShapeDtypeStruct((S, LANES), F32)],
        scratch_shapes=[pltpu.VMEM((2, tm, LANES), F32)],
        compiler_params=pltpu.CompilerParams(dimension_semantics=("arbitrary",), vmem_limit_bytes=VMEM_LIMIT),
        name="inproj",
    )(h, gain, w, rc, rm, rp, lng, lnb)


def _compress_kernel(xk_ref, xv_ref, wb_ref, w1_ref, pe_ref, b1_ref, w2_ref, b2_ref, rc_ref, rm_ref, rp_ref,
                     kc_ref, vcT_ref):
    nch = xk_ref.shape[1]

    def run(x_ref, emit):
        x = jnp.concatenate([x_ref[l] for l in range(CMP_STRIDE)], axis=1)
        c = jnp.dot(x, wb_ref[0], preferred_element_type=F32)
        cvec = jnp.dot(pe_ref[0], w1_ref[0], preferred_element_type=F32)[0:1] + b1_ref[0]
        for g in range(N_GROUPS):
            cg = c[:, 2 * CMP_HIDDEN * g:2 * CMP_HIDDEN * (g + 1)]
            h = jax.nn.gelu(cg[:, :CMP_HIDDEN] + pltpu.roll(cg[:, CMP_HIDDEN:], nch - 1, 0) + cvec)
            emit(g, jnp.dot(h.astype(BF16), w2_ref[0], preferred_element_type=F32) + b2_ref[0])

    def emit_k(g, o):
        kc_ref[g] = _rope_slab(o, rc_ref[...], rm_ref[...], rp_ref[...]).astype(BF16)

    def emit_v(g, o):
        for k in range(nch // LANES):
            rs = slice(LANES * k, LANES * (k + 1))
            vcT_ref[g, :, rs] = o[rs, :].T[0:HEAD_DIM].astype(BF16)

    @pl.when(pl.program_id(0) == 0)
    def _():
        run(xk_ref, emit_k)

    @pl.when(pl.program_id(0) == 1)
    def _():
        run(xv_ref, emit_v)


def _compress(kcx, vcx, wboth, w1, pe, b1, w2, b2, rc, rm, rp):
    _, nch, _ = kcx.shape
    cw = CMP_STRIDE * HEAD_DIM
    kv3 = lambda k: (k, 0, 0)
    fix3 = lambda k: (0, 0, 0)
    fix2 = lambda k: (0, 0)
    return pl.pallas_call(
        _compress_kernel,
        grid=(2,),
        in_specs=[_resident((CMP_STRIDE, nch, LANES), fix3), _resident((CMP_STRIDE, nch, LANES), fix3),
                  pl.BlockSpec((1, CMP_STRIDE * LANES, 2 * N_GROUPS * CMP_HIDDEN), kv3),
                  pl.BlockSpec((1, 2 * cw, CMP_HIDDEN), kv3), pl.BlockSpec((1, 8, 2 * cw), kv3),
                  pl.BlockSpec((1, 1, CMP_HIDDEN), kv3), pl.BlockSpec((1, CMP_HIDDEN, LANES), kv3),
                  pl.BlockSpec((1, 1, LANES), kv3),
                  _resident((nch, LANES), fix2), _resident((nch, LANES), fix2), _resident((nch, LANES), fix2)],
        out_specs=[pl.BlockSpec((N_GROUPS, nch, LANES), fix3), pl.BlockSpec((N_GROUPS, HEAD_DIM, nch), fix3)],
        out_shape=[jax.ShapeDtypeStruct((N_GROUPS, nch, LANES), BF16),
                   jax.ShapeDtypeStruct((N_GROUPS, HEAD_DIM, nch), BF16)],
        compiler_params=pltpu.CompilerParams(dimension_semantics=("arbitrary",), vmem_limit_bytes=VMEM_LIMIT),
        name="compress",
    )(kcx, vcx, wboth, w1, pe, b1, w2, b2, rc, rm, rp)


def _attn_kernel(q_ref, gate_ref, ksa_ref, vsT_ref, kwa_ref, vwT_ref, kc_ref, vcT_ref, cbt_ref, wb_ref, lb_ref,
                 out_ref, qaug_ref, sbuf_ref, acc_ref, ma_ref, ms_ref, ps_ref, oc_ref, ow_ref, sel_ref,
                 *, ns, ncp):
    g = pl.program_id(0)
    b = pl.program_id(1)
    t0 = pl.multiple_of(b * Q_TILE, Q_TILE)

    qf = q_ref[...].astype(F32)
    parts = []
    for j in range(GQA // 2):
        t = qf[:, LANES * j:LANES * (j + 1)].T
        parts += [t[0:HEAD_DIM], t[HEAD_DIM:]]
    qT = jnp.concatenate(parts, axis=1).astype(BF16)
    zeros = jnp.zeros((HEAD_DIM, ROWS), BF16)
    q0T = jnp.concatenate([qT, zeros], axis=0)
    qlo = jnp.where(g == 0, qT, zeros)
    qhi = jnp.where(g == 0, zeros, qT)
    qgT = jnp.concatenate([qlo, qhi], axis=0)

    ps_ref[0:8, :] = jnp.zeros((8, Q_TILE), F32)
    step = ncp // CMP_CLASSES
    cls = (b * CMP_PER_Q + CMP_PER_Q - 2) // step
    for k in range(CMP_CLASSES):
        @pl.when(cls == k)
        def _(nk=(k + 1) * step):
            cb = cbt_ref[pl.ds(pl.multiple_of(ncp - b * CMP_PER_Q, 8), nk), :]
            s = jnp.dot(kc_ref[0, 0:nk, :], q0T, preferred_element_type=F32) + _tile_heads(cb)
            m = _colmax(s)
            e = jnp.exp2(s - m)
            inv = jnp.where(m > NEG * 0.5, 1.0 / jnp.maximum(jnp.sum(e, axis=0, keepdims=True), 1e-30), 0.0)
            p = e * inv
            oc_ref[...] = jnp.dot(vcT_ref[0, :, 0:nk], p.astype(BF16), preferred_element_type=F32)
            ps = p[:, 0:Q_TILE]
            for r in range(1, GQA):
                ps = ps + p[:, r * Q_TILE:(r + 1) * Q_TILE]
            ps_ref[8:8 + nk, :] = ps
            if nk < ncp:
                ps_ref[8 + nk:, :] = jnp.zeros((ncp - nk, Q_TILE), F32)

    ratio = SEL_LEN // CMP_STRIDE
    imp = sum(ps_ref[pl.ds(8 + d, ns, stride=ratio), :] for d in IMP_TAPS)

    wk = WINDOW + Q_TILE
    ws = pl.multiple_of(jnp.maximum(t0 - WINDOW, 0), Q_TILE)
    wb = wb_ref[jnp.minimum(b, WIN_EARLY)]
    s = jnp.dot(kwa_ref[pl.ds(ws, wk), :], qgT, preferred_element_type=F32) + _tile_heads(wb)
    pw = jnp.exp2(s - _colmax(s)).astype(BF16)
    ow = jnp.dot(vwT_ref[0, :, pl.ds(ws, wk)], pw, preferred_element_type=F32)
    ow_ref[...] = ow[0:HEAD_DIM] / jnp.maximum(ow[HEAD_DIM:HEAD_DIM + 1], 1e-30)

    s = (jnp.dot(ksa_ref[pl.ds(t0, Q_TILE), 0:LANES], qgT, preferred_element_type=F32)
         + _tile_heads(lb_ref[...]))
    m0 = _colmax(s)
    p0 = jnp.exp2(s - m0).astype(BF16)
    acc_ref[...] = jnp.dot(vsT_ref[0, :, pl.ds(t0, Q_TILE)], p0, preferred_element_type=F32)
    ma_ref[...] = m0
    ms_ref[...] = m0

    blk = lax.broadcasted_iota(jnp.int32, (ns, Q_TILE), 0)
    cur = (t0 + lax.broadcasted_iota(jnp.int32, (ns, Q_TILE), 1)) // SEL_LEN
    valid = blk <= cur
    forced = valid & ((blk == 0) | (blk == cur) | (blk == cur - 1))
    rank0 = jnp.where(valid & jnp.logical_not(forced), imp, NEG)
    blk_f = blk.astype(F32)
    taken = -(2.0 ** 126)

    def rank(break_ties):
        score = rank0
        for _ in range(N_SELECT - N_FORCED):
            mx = _colmax(score)
            if break_ties:
                idx = jnp.min(jnp.where(score == mx, blk_f, float(ns)), axis=0, keepdims=True)
                hit = blk_f == idx
            else:
                hit = score == mx
            score = jnp.where(hit, taken, score)
        return (score == taken) & (rank0 > NEG * 0.5)

    picked = rank(False)
    count = jnp.sum(jnp.where(picked, 1.0, 0.0), axis=0, keepdims=True)
    want = jnp.clip(cur[0:1] - (N_FORCED - 1), 0, N_SELECT - N_FORCED).astype(F32)
    sel_ref[...] = jnp.where(forced | picked, 1.0, 0.0)

    @pl.when(jnp.logical_not(jnp.all(count == want)))
    def _():
        sel_ref[...] = jnp.where(forced | rank(True), 1.0, 0.0)

    sel = sel_ref[...] > 0.5
    bias = jnp.where(sel & (blk < b * SEL_PER_Q), 0.0, NEG).astype(BF16)
    for w in range(ns // BIAS_WIN):
        qaug_ref[w, 0:2 * HEAD_DIM, :] = qgT
        qaug_ref[w, 2 * HEAD_DIM:, :] = _tile_heads(bias[BIAS_WIN * w:BIAS_WIN * (w + 1), :])

    n_tiles = jnp.maximum((t0 + KV_TILE - 1) // KV_TILE, 1)
    tiles_per_win = BIAS_WIN * SEL_LEN // KV_TILE

    def scores(t, slot):
        start = pl.multiple_of(t * KV_TILE, KV_TILE)
        s = jnp.dot(ksa_ref[pl.ds(start, KV_TILE), :], qaug_ref[t // tiles_per_win],
                    preferred_element_type=F32)
        sbuf_ref[slot] = s
        ms_ref[...] = jnp.maximum(ms_ref[...], _colmax(s))

    def consume(t, slot, m_s):
        start = pl.multiple_of(t * KV_TILE, KV_TILE)
        alpha = jnp.exp2(ma_ref[...] - m_s)
        pp = jnp.exp2(sbuf_ref[slot] - m_s).astype(BF16)
        acc_ref[...] = alpha * acc_ref[...] + jnp.dot(vsT_ref[0, :, pl.ds(start, KV_TILE)], pp,
                                                      preferred_element_type=F32)
        ma_ref[...] = m_s

    def step_(t, slot):
        m_s = ms_ref[...]
        scores(t + 1, 1 - slot)
        consume(t, slot, m_s)

    scores(0, 0)

    def body(i, carry):
        step_(2 * i, 0)
        step_(2 * i + 1, 1)
        return carry

    n_steps = n_tiles - 1
    lax.fori_loop(0, n_steps // 2, body, 0)

    @pl.when(n_steps % 2 == 1)
    def _():
        step_(n_steps - 1, 0)

    consume(n_steps, n_steps % 2, ms_ref[...])

    owT = ow_ref[...]
    acc = acc_ref[...]
    osT = acc[0:HEAD_DIM] / jnp.maximum(acc[HEAD_DIM:HEAD_DIM + 1], 1e-30)

    gall = gate_ref[...].T
    gg = jnp.where(g == 0, gall[0:GATE_SLOT], gall[GATE_SLOT:2 * GATE_SLOT])
    gt = lambda br: jnp.concatenate([gg[br * GQA + r:br * GQA + r + 1] for r in range(GQA)], axis=1)
    oT = gt(0) * oc_ref[...] + gt(1) * osT + gt(2) * owT
    for j in range(GQA // 2):
        pair = jnp.concatenate([oT[:, Q_TILE * 2 * j:Q_TILE * (2 * j + 1)],
                                oT[:, Q_TILE * (2 * j + 1):Q_TILE * (2 * j + 2)]], axis=0)
        out_ref[:, LANES * j:LANES * (j + 1)] = pair.T


def _attention(q, gates, ksa, vsT, kwa, vwT, kc, vcT, cbt, wb, lb):
    S = q.shape[0]
    nqb = S // Q_TILE
    ncp = kc.shape[1]
    ns = S // SEL_LEN
    gw = GQA * HEAD_DIM
    fix = lambda g, b: (0, 0)
    per_g3 = lambda g, b: (g, 0, 0)
    kernel = functools.partial(_attn_kernel, ns=ns, ncp=ncp)
    return pl.pallas_call(
        kernel,
        grid=(N_GROUPS, nqb),
        in_specs=[pl.BlockSpec((Q_TILE, gw), lambda g, b: (b, g)), pl.BlockSpec((Q_TILE, LANES), lambda g, b: (b, 0)),
                  _resident((S, 2 * LANES), fix), pl.BlockSpec((1, V_ROWS, S), per_g3),
                  _resident((S, LANES), fix), pl.BlockSpec((1, V_ROWS, S), per_g3),
                  pl.BlockSpec((1, ncp, LANES), per_g3), pl.BlockSpec((1, HEAD_DIM, ncp), per_g3),
                  _resident((2 * ncp, Q_TILE), fix),
                  _resident((WIN_EARLY + 1, WINDOW + Q_TILE, Q_TILE), lambda g, b: (0, 0, 0)),
                  _resident((Q_TILE, Q_TILE), fix)],
        out_specs=pl.BlockSpec((Q_TILE, gw), lambda g, b: (b, g)),
        out_shape=jax.ShapeDtypeStruct((S, ATTN_WIDTH), F32),
        scratch_shapes=[pltpu.VMEM((ns // BIAS_WIN, 2 * LANES, ROWS), BF16),
                        pltpu.VMEM((2, KV_TILE, ROWS), F32),
                        pltpu.VMEM((V_ROWS, ROWS), F32),
                        pltpu.VMEM((1, ROWS), F32), pltpu.VMEM((1, ROWS), F32),
                        pltpu.VMEM((8 + ncp, Q_TILE), F32),
                        pltpu.VMEM((HEAD_DIM, ROWS), F32),
                        pltpu.VMEM((HEAD_DIM, ROWS), F32),
                        pltpu.VMEM((ns, Q_TILE), F32)],
        compiler_params=pltpu.CompilerParams(dimension_semantics=("arbitrary", "arbitrary"),
                                             vmem_limit_bytes=VMEM_LIMIT),
        name="nsa_attention",
    )(q, gates, ksa, vsT, kwa, vwT, kc, vcT, cbt, wb, lb)


def _gmlp_kernel(u_ref, v_ref, ws_ref, bs_ref, g_ref, out_ref, wsm_ref, *, chunks):
    @pl.when(pl.program_id(0) == 0)
    def _():
        r = lax.broadcasted_iota(jnp.int32, (CHUNK, CHUNK), 0)
        c = lax.broadcasted_iota(jnp.int32, (CHUNK, CHUNK), 1)
        for g in range(MLP_GROUPS):
            wsm_ref[g] = jnp.where(c <= r, ws_ref[g], 0.0).astype(BF16)

    lane = lax.broadcasted_iota(jnp.int32, (CHUNK, LANES), 1)
    left = lane < HEAD_DIM
    for ci in range(chunks):
        rows = slice(ci * CHUNK, (ci + 1) * CHUNK)
        parts = []
        for pr in range(MLP_GROUPS // 2):
            vp = v_ref[rows, pr * LANES:(pr + 1) * LANES]
            va = jnp.where(left, vp, jnp.zeros_like(vp))
            vb = jnp.where(left, jnp.zeros_like(vp), vp)
            parts.append(jnp.dot(wsm_ref[2 * pr], va, preferred_element_type=F32)
                         + jnp.dot(wsm_ref[2 * pr + 1], vb, preferred_element_type=F32))
        mixed = jnp.concatenate(parts, axis=1) + bs_ref[...]
        y = u_ref[rows, :].astype(F32) * mixed
        out_ref[rows, :] = _rms(y, g_ref[...]).astype(BF16)


def _gmlp(u, v, ws, bs_exp, gain, chunks=4):
    S = u.shape[0]
    tm = CHUNK * chunks
    row = lambda i: (i, 0)
    return pl.pallas_call(
        functools.partial(_gmlp_kernel, chunks=chunks),
        grid=(S // tm,),
        in_specs=[pl.BlockSpec((tm, MLP_WIDTH), row), pl.BlockSpec((tm, MLP_WIDTH), row),
                  _resident((MLP_GROUPS, CHUNK, CHUNK), lambda i: (0, 0, 0)),
                  _resident((CHUNK, MLP_WIDTH), lambda i: (0, 0)),
                  _resident((1, MLP_WIDTH), lambda i: (0, 0))],
        out_specs=pl.BlockSpec((tm, MLP_WIDTH), row),
        out_shape=jax.ShapeDtypeStruct((S, MLP_WIDTH), BF16),
        scratch_shapes=[pltpu.VMEM((MLP_GROUPS, CHUNK, CHUNK), BF16)],
        compiler_params=pltpu.CompilerParams(dimension_semantics=("arbitrary",), vmem_limit_bytes=VMEM_LIMIT),
        name="gmlp",
    )(u, v, ws, bs_exp, gain)


def _outproj_kernel(h_ref, a_ref, m_ref, ag_ref, wa_ref, wm_ref, pg_ref, out_ref):
    an = _rms(a_ref[...], ag_ref[...]).astype(BF16)
    mix = (jnp.dot(an, wa_ref[...], preferred_element_type=F32)
           + jnp.dot(m_ref[...], wm_ref[...], preferred_element_type=F32))
    out_ref[...] = h_ref[...] + _rms(mix, pg_ref[...])


def _outproj(h, attn, mlpn, ag, wa, wm, pg, tm=512):
    S, D = h.shape
    row = lambda i: (i, 0)
    fix = lambda i: (0, 0)
    return pl.pallas_call(
        _outproj_kernel,
        grid=(S // tm,),
        in_specs=[pl.BlockSpec((tm, D), row), pl.BlockSpec((tm, ATTN_WIDTH), row), pl.BlockSpec((tm, MLP_WIDTH), row),
                  _resident((1, ATTN_WIDTH), fix), _resident((ATTN_WIDTH, D), fix),
                  _resident((MLP_WIDTH, D), fix), _resident((1, D), fix)],
        out_specs=pl.BlockSpec((tm, D), row),
        out_shape=jax.ShapeDtypeStruct((S, D), F32),
        compiler_params=pltpu.CompilerParams(dimension_semantics=("arbitrary",), vmem_limit_bytes=VMEM_LIMIT),
        name="outproj",
    )(h, attn, mlpn, ag, wa, wm, pg)


HALO = BF16_ROWS


def _ffn_kernel(h_ref, halo_ref, p_ref, g_ref, wup_ref, cw_ref, cb_ref, wd_ref, pg_ref, plg_ref, wpg_ref, wpp_ref,
                out_ref, acc_ref, *, fc):
    dff = wd_ref.shape[0]
    h = h_ref[...]
    hn = _rms(halo_ref[...], g_ref[...])
    hn = jnp.where(pl.program_id(0) == 0, 0.0, hn)
    xn = jnp.concatenate([hn.astype(BF16), _rms(h, g_ref[...]).astype(BF16)], axis=0)

    def conv(lo, n):
        hh = jnp.dot(xn, wup_ref[:, lo:lo + n], preferred_element_type=F32)
        cw = cw_ref[:, lo:lo + n]
        y = cb_ref[:, lo:lo + n] + pltpu.roll(hh, 2, 0) * cw[0:1] + pltpu.roll(hh, 1, 0) * cw[1:2] + hh * cw[2:3]
        return y[HALO:, :]

    for c0 in range(0, dff, fc):
        n = min(fc, dff - c0)
        act = (jax.nn.silu(conv(c0, n)) * conv(dff + c0, n)).astype(BF16)
        part = jnp.dot(act, wd_ref[c0:c0 + n, :], preferred_element_type=F32)
        if c0 == 0:
            acc_ref[...] = part
        else:
            acc_ref[...] += part

    h2 = h + _rms(acc_ref[...], pg_ref[...])
    gate = jax.nn.sigmoid(jnp.dot(_rms(h2, plg_ref[...]).astype(BF16), wpg_ref[...], preferred_element_type=F32))
    proj = jnp.dot(p_ref[...].astype(BF16), wpp_ref[...], preferred_element_type=F32)
    out_ref[...] = h2 + gate * proj


def _ffn(h, p, gain, w_up, conv_w, conv_b, w_down, pg, plg, wpg, wpp, tm=512, fc=1024):
    S, D = h.shape
    dff = w_down.shape[0]
    P = p.shape[1]
    row = lambda i: (i, 0)
    fix = lambda i: (0, 0)
    halo = lambda i: (jnp.maximum(i * (tm // HALO) - 1, 0), 0)
    return pl.pallas_call(
        functools.partial(_ffn_kernel, fc=fc),
        grid=(S // tm,),
        in_specs=[pl.BlockSpec((tm, D), row), pl.BlockSpec((HALO, D), halo), pl.BlockSpec((tm, P), row),
                  _resident((1, D), fix), _resident((D, 2 * dff), fix), _resident((CONV_WIDTH, 2 * dff), fix),
                  _resident((1, 2 * dff), fix), _resident((dff, D), fix), _resident((1, D), fix),
                  _resident((1, D), fix), _resident((D, D), fix), _resident((P, D), fix)],
        out_specs=pl.BlockSpec((tm, D), row),
        out_shape=jax.ShapeDtypeStruct((S, D), F32),
        scratch_shapes=[pltpu.VMEM((tm, D), F32)],
        compiler_params=pltpu.CompilerParams(dimension_semantics=("arbitrary",), vmem_limit_bytes=VMEM_LIMIT),
        name="convffn_ple",
    )(h, h, p, gain, w_up, conv_w, conv_b, w_down, pg, plg, wpg, wpp)


def _rope_tables(pos):
    half = ROT_DIM // 2
    inv = ROPE_THETA ** (-np.arange(half, dtype=np.float64) / half)
    ang = np.asarray(pos, np.float64)[:, None] * inv[None, :]
    cos, sin = jnp.asarray(np.cos(ang), dtype=F32), jnp.asarray(np.sin(ang), dtype=F32)
    n = ang.shape[0]
    one = jnp.ones((n, HEAD_DIM - ROT_DIM), F32)
    zero = jnp.zeros((n, HEAD_DIM - ROT_DIM), F32)
    zh = jnp.zeros((n, half), F32)
    c = jnp.concatenate([cos, cos, one], axis=1)
    sm = jnp.concatenate([-sin, zh, zero], axis=1)
    sp = jnp.concatenate([zh, sin, zero], axis=1)
    dup = lambda t: jnp.concatenate([t, t], axis=1)
    return dup(c), dup(sm), dup(sp)


def _mask_tables(ncp):
    qo = np.arange(Q_TILE)[None, :]
    rel = (np.arange(2 * ncp) - ncp)[:, None]
    cmp_ok = rel * CMP_STRIDE + CMP_LEN - 1 <= qo
    r = np.arange(WINDOW + Q_TILE)[:, None]
    win_ok = []
    for v in range(WIN_EARLY):
        tq = v * Q_TILE + qo
        win_ok.append((r <= tq) & (r > tq - WINDOW))
    win_ok.append((r - WINDOW <= qo) & (r > qo))
    loc_ok = np.arange(Q_TILE)[:, None] <= qo
    tab = lambda ok: jnp.asarray(np.where(ok, 0.0, NEG), dtype=F32)
    return tab(cmp_ok), tab(np.stack(win_ok)), tab(loc_ok)


def _gate_columns():
    src = np.full((LANES,), -1)
    for g in range(N_GROUPS):
        for br in range(N_BRANCH):
            for r in range(GQA):
                src[g * GATE_SLOT + br * GQA + r] = (g * GQA + r) * N_BRANCH + br
    return src


def _layer(h, p, prm, tabs):
    S, D = h.shape
    rc, rm, rp, crc, crm, crp, cbt, wb, lb = tabs

    sizes = [ATTN_WIDTH] + [KV_WIDTH] * 6 + [N_HEADS * N_BRANCH, MLP_WIDTH, MLP_WIDTH]
    offs = np.concatenate([[0], np.cumsum(sizes)])
    w_in = prm["w_in"]
    seg = lambda k: w_in[:, offs[k]:offs[k + 1]]
    src = _gate_columns()
    wg = jnp.where(jnp.asarray(src >= 0)[None, :], seg(7)[:, np.maximum(src, 0)], 0.0)
    w_cat = jnp.concatenate([seg(0)] + [seg(k) for k in range(1, 7)] + [seg(8), seg(9), wg], axis=1).astype(BF16)

    q, ksa, kwa, vsT, vwT, kcx, vcx, u, v, gates = _inproj(h, prm["pre_mix_g"][None], w_cat, rc, rm, rp,
                                                           prm["gmlp_ln_g"][None], prm["gmlp_ln_b"][None])

    w1 = prm["cmp_w1"].astype(BF16)
    half = CMP_STRIDE * HEAD_DIM
    w1l = jnp.concatenate([w1[:, :half], w1[:, half:]], axis=2).reshape(2, CMP_STRIDE, HEAD_DIM, 2 * CMP_HIDDEN)
    zl = jnp.zeros_like(w1l)
    wboth = jnp.concatenate([jnp.concatenate([w1l, zl], axis=3), jnp.concatenate([zl, w1l], axis=3)], axis=2)
    wboth = wboth.reshape(2, CMP_STRIDE * LANES, 2 * N_GROUPS * CMP_HIDDEN)
    pe = jnp.broadcast_to(prm["cmp_pe"].reshape(2, 1, CMP_LEN * HEAD_DIM), (2, 8, CMP_LEN * HEAD_DIM)).astype(BF16)
    w2 = jnp.pad(prm["cmp_w2"], ((0, 0), (0, 0), (0, LANES - HEAD_DIM))).astype(BF16)
    b2 = jnp.pad(prm["cmp_b2"], ((0, 0), (0, LANES - HEAD_DIM)))[:, None]
    kc, vcT = _compress(kcx, vcx, wboth, w1, pe, prm["cmp_b1"][:, None], w2, b2, crc, crm, crp)

    attn = _attention(q, gates, ksa, vsT, kwa, vwT, kc, vcT, cbt, wb, lb)

    bs_exp = jnp.repeat(prm["gmlp_bs"].T, HEAD_DIM, axis=1)
    mlpn = _gmlp(u, v, prm["gmlp_ws"], bs_exp, prm["mlp_out_g"][None])

    w_o = prm["w_o"].astype(BF16)
    h = _outproj(h, attn, mlpn, prm["attn_out_g"][None], w_o[:ATTN_WIDTH], w_o[ATTN_WIDTH:], prm["post_mix_g"][None])
    h = _ffn(h, p, prm["pre_ffn_g"][None], prm["w_up"].astype(BF16), prm["conv_w"], prm["conv_b"][None],
             prm["w_down"].astype(BF16), prm["post_ffn_g"][None], prm["ple_norm_g"][None],
             prm["w_ple_gate"].astype(BF16), prm["w_ple_proj"].astype(BF16))
    return h


def kernel(x, p, pre_mix_g, w_in, cmp_pe, cmp_w1, cmp_b1, cmp_w2, cmp_b2, gmlp_ln_g, gmlp_ln_b, gmlp_ws, gmlp_bs,
           attn_out_g, mlp_out_g, w_o, post_mix_g, pre_ffn_g, w_up, conv_w, conv_b, w_down, post_ffn_g,
           ple_norm_g, w_ple_gate, w_ple_proj):
    B, S, D = x.shape
    depth = p.shape[0]
    assert S % (BIAS_WIN * SEL_LEN) == 0 and D % LANES == 0
    nch = S // CMP_STRIDE
    rc, rm, rp = _rope_tables(np.arange(S))
    crc, crm, crp = _rope_tables(np.arange(nch) * CMP_STRIDE + CMP_LEN - 1)
    tabs = (rc, rm, rp, crc, crm, crp) + _mask_tables(nch)
    stacked = dict(pre_mix_g=pre_mix_g, w_in=w_in, cmp_pe=cmp_pe, cmp_w1=cmp_w1, cmp_b1=cmp_b1, cmp_w2=cmp_w2,
                   cmp_b2=cmp_b2, gmlp_ln_g=gmlp_ln_g, gmlp_ln_b=gmlp_ln_b, gmlp_ws=gmlp_ws, gmlp_bs=gmlp_bs,
                   attn_out_g=attn_out_g, mlp_out_g=mlp_out_g, w_o=w_o, post_mix_g=post_mix_g, pre_ffn_g=pre_ffn_g,
                   w_up=w_up, conv_w=conv_w, conv_b=conv_b, w_down=w_down, post_ffn_g=post_ffn_g,
                   ple_norm_g=ple_norm_g, w_ple_gate=w_ple_gate, w_ple_proj=w_ple_proj)
    outs = []
    for bi in range(B):
        h = x[bi]
        for i in range(depth):
            h = _layer(h, p[i, bi], {k: v[i] for k, v in stacked.items()}, tabs)
        outs.append(h)
    return jnp.stack(outs)
```

```python
import functools
import math

import numpy as np
import jax
import jax.numpy as jnp
from jax import lax
from jax.experimental import pallas as pl
from jax.experimental.pallas import tpu as pltpu

F32 = jnp.float32
BF16 = jnp.bfloat16

HEAD_DIM = 64
N_HEADS = 8
N_GROUPS = 2
GQA = N_HEADS // N_GROUPS
N_BRANCH = 3
ATTN_WIDTH = N_HEADS * HEAD_DIM
MLP_GROUPS = 8
MLP_WIDTH = MLP_GROUPS * HEAD_DIM
KV_WIDTH = N_GROUPS * HEAD_DIM
ROT_DIM = HEAD_DIM // 4
ROPE_THETA = 500000.0
CMP_LEN = 32
CMP_STRIDE = 16
CMP_HIDDEN = 256
SEL_LEN = 64
N_SELECT = 16
WINDOW = 512
CHUNK = 128
CONV_WIDTH = 3
NORM_EPS = 1e-6
NEG = -1e30
SCALE = HEAD_DIM ** -0.5
QSCALE = SCALE * math.log2(math.e)

LANES = 128
BF16_ROWS = 16
Q_TILE = 128
ROWS = GQA * Q_TILE
N_COLS = N_GROUPS * ROWS
KV_TILE = 1024
BIAS_WIN = 128
V_ROWS = HEAD_DIM + BF16_ROWS
CMP_PER_Q = Q_TILE // CMP_STRIDE
SEL_PER_Q = Q_TILE // SEL_LEN
CMP_CLASSES = 4
N_FORCED = 3
GATE_SLOT = 16
WIN_EARLY = WINDOW // Q_TILE
IMP_TAPS = range(-1, SEL_LEN // CMP_STRIDE)
VMEM_LIMIT = 56 * 1024 * 1024


def _rms(x, gain):
    return x * lax.rsqrt(jnp.mean(x * x, axis=-1, keepdims=True) + NORM_EPS) * gain


def _rope_slab(x, c, sm, sp):
    return x * c + pltpu.roll(x, LANES - ROT_DIM // 2, 1) * sm + pltpu.roll(x, ROT_DIM // 2, 1) * sp


def _colmax(s):
    return jnp.max(s, axis=0, keepdims=True)


def _tile_heads(x):
    return jnp.concatenate([x] * GQA, axis=1)


def _resident(shape, index_map):
    return pl.BlockSpec(shape, index_map, pipeline_mode=pl.Buffered(1))


def _inproj_kernel(x_ref, g_ref, w_ref, rc_ref, rm_ref, rp_ref, lng_ref, lnb_ref,
                   q_ref, ksa_ref, kwa_ref, vsT_ref, vwT_ref, kcx_ref, vcx_ref, u_ref, v_ref, gate_ref, craw_ref):
    tm = x_ref.shape[0]
    a = _rms(x_ref[...], g_ref[...]).astype(BF16)
    c, sm, sp = rc_ref[...], rm_ref[...], rp_ref[...]
    zq = jnp.dot(a, w_ref[:, 0:ATTN_WIDTH], preferred_element_type=F32)
    for j in range(ATTN_WIDTH // LANES):
        sl = slice(LANES * j, LANES * (j + 1))
        q_ref[:, sl] = (_rope_slab(zq[:, sl], c, sm, sp) * QSCALE).astype(BF16)
    o = ATTN_WIDTH
    zkv = jnp.dot(a, w_ref[:, o:o + 6 * KV_WIDTH], preferred_element_type=F32)
    slab = lambda j: zkv[:, LANES * j:LANES * (j + 1)]

    craw_ref[0] = slab(0)
    craw_ref[1] = slab(1)
    nch = tm // CMP_STRIDE
    for l in range(CMP_STRIDE):
        kcx_ref[l] = craw_ref[0, pl.ds(l, nch, stride=CMP_STRIDE), :].astype(BF16)
        vcx_ref[l] = craw_ref[1, pl.ds(l, nch, stride=CMP_STRIDE), :].astype(BF16)

    pos = pl.program_id(0) * tm + lax.broadcasted_iota(jnp.int32, (tm, BIAS_WIN), 0)
    lane = lax.broadcasted_iota(jnp.int32, (tm, BIAS_WIN), 1)
    ksa_ref[:, 0:LANES] = _rope_slab(slab(2), c, sm, sp).astype(BF16)
    ksa_ref[:, LANES:] = jnp.where((pos // SEL_LEN) % BIAS_WIN == lane, 1.0, 0.0).astype(BF16)
    kwa_ref[...] = _rope_slab(slab(4), c, sm, sp).astype(BF16)
    for k in range(tm // LANES):
        rs = slice(LANES * k, LANES * (k + 1))
        for ref, sl in ((vsT_ref, slab(3)), (vwT_ref, slab(5))):
            t = sl[rs, :].T.astype(BF16)
            for g in range(N_GROUPS):
                ref[g, 0:HEAD_DIM, rs] = t[HEAD_DIM * g:HEAD_DIM * (g + 1)]
                ref[g, HEAD_DIM:, rs] = jnp.ones((BF16_ROWS, LANES), BF16)

    o += 6 * KV_WIDTH
    zu = jnp.dot(a, w_ref[:, o:o + MLP_WIDTH], preferred_element_type=F32)
    u_ref[...] = jax.nn.gelu(zu).astype(BF16)
    o += MLP_WIDTH
    zv = jax.nn.gelu(jnp.dot(a, w_ref[:, o:o + MLP_WIDTH], preferred_element_type=F32))
    mu = jnp.mean(zv, axis=-1, keepdims=True)
    d = zv - mu
    var = jnp.mean(d * d, axis=-1, keepdims=True)
    v_ref[...] = (d * lax.rsqrt(var + NORM_EPS) * lng_ref[...] + lnb_ref[...]).astype(BF16)
    o += MLP_WIDTH
    zg = jnp.dot(a, w_ref[:, o:o + LANES], preferred_element_type=F32)
    gate_ref[...] = jax.nn.sigmoid(zg)


def _inproj(h, gain, w, rc, rm, rp, lng, lnb, tm=512):
    S, D = h.shape
    N = w.shape[1]
    nch = S // CMP_STRIDE
    row = lambda i: (i, 0)
    fix = lambda i: (0, 0)
    chunked = lambda i: (0, i, 0)
    bf = lambda *shape: jax.ShapeDtypeStruct(shape, BF16)
    return pl.pallas_call(
        _inproj_kernel,
        grid=(S // tm,),
        in_specs=[pl.BlockSpec((tm, D), row), _resident((1, D), fix), _resident((D, N), fix),
                  pl.BlockSpec((tm, LANES), row), pl.BlockSpec((tm, LANES), row), pl.BlockSpec((tm, LANES), row),
                  _resident((1, MLP_WIDTH), fix), _resident((1, MLP_WIDTH), fix)],
        out_specs=[pl.BlockSpec((tm, ATTN_WIDTH), row), pl.BlockSpec((tm, 2 * LANES), row),
                   pl.BlockSpec((tm, LANES), row), pl.BlockSpec((N_GROUPS, V_ROWS, tm), lambda i: (0, 0, i)),
                   pl.BlockSpec((N_GROUPS, V_ROWS, tm), lambda i: (0, 0, i)),
                   pl.BlockSpec((CMP_STRIDE, tm // CMP_STRIDE, LANES), chunked),
                   pl.BlockSpec((CMP_STRIDE, tm // CMP_STRIDE, LANES), chunked),
                   pl.BlockSpec((tm, MLP_WIDTH), row), pl.BlockSpec((tm, MLP_WIDTH), row),
                   pl.BlockSpec((tm, LANES), row)],
        out_shape=[bf(S, ATTN_WIDTH), bf(S, 2 * LANES), bf(S, LANES), bf(N_GROUPS, V_ROWS, S), bf(N_GROUPS, V_ROWS, S),
                   bf(CMP_STRIDE, nch, LANES), bf(CMP_STRIDE, nch, LANES), bf(S, MLP_WIDTH), bf(S, MLP_WIDTH),
                   jax.ShapeDtypeStruct((S, LANES), F32)],
        scratch_shapes=[pltpu.VMEM((2, tm, LANES), F32)],
        compiler_params=pltpu.CompilerParams(dimension_semantics=("arbitrary",), vmem_limit_bytes=VMEM_LIMIT),
        name="inproj",
    )(h, gain, w, rc, rm, rp, lng, lnb)


def _compress_kernel(xk_ref, xv_ref, wb_ref, w1_ref, pe_ref, b1_ref, w2_ref, b2_ref, rc_ref, rm_ref, rp_ref,
                     kc_ref, vcT_ref):
    nch = xk_ref.shape[1]

    def run(x_ref, emit):
        x = jnp.concatenate([x_ref[l] for l in range(CMP_STRIDE)], axis=1)
        c = jnp.dot(x, wb_ref[0], preferred_element_type=F32)
        cvec = jnp.dot(pe_ref[0], w1_ref[0], preferred_element_type=F32)[0:1] + b1_ref[0]
        for g in range(N_GROUPS):
            cg = c[:, 2 * CMP_HIDDEN * g:2 * CMP_HIDDEN * (g + 1)]
            h = jax.nn.gelu(cg[:, :CMP_HIDDEN] + pltpu.roll(cg[:, CMP_HIDDEN:], nch - 1, 0) + cvec)
            emit(g, jnp.dot(h.astype(BF16), w2_ref[0], preferred_element_type=F32) + b2_ref[0])

    def emit_k(g, o):
        kc_ref[g] = _rope_slab(o, rc_ref[...], rm_ref[...], rp_ref[...]).astype(BF16)

    def emit_v(g, o):
        for k in range(nch // LANES):
            rs = slice(LANES * k, LANES * (k + 1))
            vcT_ref[g, :, rs] = o[rs, :].T[0:HEAD_DIM].astype(BF16)

    @pl.when(pl.program_id(0) == 0)
    def _():
        run(xk_ref, emit_k)

    @pl.when(pl.program_id(0) == 1)
    def _():
        run(xv_ref, emit_v)


def _compress(kcx, vcx, wboth, w1, pe, b1, w2, b2, rc, rm, rp):
    _, nch, _ = kcx.shape
    cw = CMP_STRIDE * HEAD_DIM
    kv3 = lambda k: (k, 0, 0)
    fix3 = lambda k: (0, 0, 0)
    fix2 = lambda k: (0, 0)
    return pl.pallas_call(
        _compress_kernel,
        grid=(2,),
        in_specs=[_resident((CMP_STRIDE, nch, LANES), fix3), _resident((CMP_STRIDE, nch, LANES), fix3),
                  pl.BlockSpec((1, CMP_STRIDE * LANES, 2 * N_GROUPS * CMP_HIDDEN), kv3),
                  pl.BlockSpec((1, 2 * cw, CMP_HIDDEN), kv3), pl.BlockSpec((1, 8, 2 * cw), kv3),
                  pl.BlockSpec((1, 1, CMP_HIDDEN), kv3), pl.BlockSpec((1, CMP_HIDDEN, LANES), kv3),
                  pl.BlockSpec((1, 1, LANES), kv3),
                  _resident((nch, LANES), fix2), _resident((nch, LANES), fix2), _resident((nch, LANES), fix2)],
        out_specs=[pl.BlockSpec((N_GROUPS, nch, LANES), fix3), pl.BlockSpec((N_GROUPS, HEAD_DIM, nch), fix3)],
        out_shape=[jax.ShapeDtypeStruct((N_GROUPS, nch, LANES), BF16),
                   jax.ShapeDtypeStruct((N_GROUPS, HEAD_DIM, nch), BF16)],
        compiler_params=pltpu.CompilerParams(dimension_semantics=("arbitrary",), vmem_limit_bytes=VMEM_LIMIT),
        name="compress",
    )(kcx, vcx, wboth, w1, pe, b1, w2, b2, rc, rm, rp)


def _attn_kernel(q_ref, gate_ref, ksa_ref, vsT_ref, kwa_ref, vwT_ref, kc_ref, vcT_ref, cbt_ref, wb_ref, lb_ref,
                 out_ref, qaug_ref, sbuf_ref, acc_ref, ma_ref, ms_ref, ps_ref, oc_ref, ow_ref, sel_ref,
                 *, ns, ncp):
    b = pl.program_id(0)
    t0 = pl.multiple_of(b * Q_TILE, Q_TILE)
    gcols = lambda g: slice(ROWS * g, ROWS * (g + 1))
    gq = lambda g: slice(Q_TILE * g, Q_TILE * (g + 1))
    per_group = lambda f: jnp.concatenate([f(g) for g in range(N_GROUPS)], axis=1)

    qf = q_ref[...].astype(F32)
    parts = []
    for j in range(N_HEADS // 2):
        t = qf[:, LANES * j:LANES * (j + 1)].T
        parts += [t[0:HEAD_DIM], t[HEAD_DIM:]]
    qT = jnp.concatenate(parts, axis=1).astype(BF16)
    zeros = jnp.zeros((HEAD_DIM, ROWS), BF16)
    qgT = jnp.concatenate(
        [jnp.concatenate([qT[:, gcols(g)] if gg == g else zeros for gg in range(N_GROUPS)], axis=1)
         for g in range(N_GROUPS)], axis=0)

    step = ncp // CMP_CLASSES
    cls = (b * CMP_PER_Q + CMP_PER_Q - 2) // step
    for k in range(CMP_CLASSES):
        @pl.when(cls == k)
        def _(nk=(k + 1) * step):
            cb = _tile_heads(cbt_ref[pl.ds(pl.multiple_of(ncp - b * CMP_PER_Q, 8), nk), :])
            for g in range(N_GROUPS):
                q0T = jnp.concatenate([qT[:, gcols(g)], zeros], axis=0)
                s = jnp.dot(kc_ref[g, 0:nk, :], q0T, preferred_element_type=F32) + cb
                m = _colmax(s)
                e = jnp.exp2(s - m)
                inv = jnp.where(m > NEG * 0.5, 1.0 / jnp.maximum(jnp.sum(e, axis=0, keepdims=True), 1e-30), 0.0)
                p = e * inv
                oc_ref[:, gcols(g)] = jnp.dot(vcT_ref[g, :, 0:nk], p.astype(BF16), preferred_element_type=F32)
                ps = p[:, 0:Q_TILE]
                for r in range(1, GQA):
                    ps = ps + p[:, r * Q_TILE:(r + 1) * Q_TILE]
                ps_ref[g, 0:8, :] = jnp.zeros((8, Q_TILE), F32)
                ps_ref[g, 8:8 + nk, :] = ps
                if nk < ncp:
                    ps_ref[g, 8 + nk:, :] = jnp.zeros((ncp - nk, Q_TILE), F32)

    ratio = SEL_LEN // CMP_STRIDE
    imp = per_group(lambda g: sum(ps_ref[g, pl.ds(8 + d, ns, stride=ratio), :] for d in IMP_TAPS))

    wk = WINDOW + Q_TILE
    ws = pl.multiple_of(jnp.maximum(t0 - WINDOW, 0), Q_TILE)
    wb = _tile_heads(wb_ref[jnp.minimum(b, WIN_EARLY)])
    s = jnp.dot(kwa_ref[pl.ds(ws, wk), :], qgT, preferred_element_type=F32) + per_group(lambda g: wb)
    pw = jnp.exp2(s - _colmax(s)).astype(BF16)
    for g in range(N_GROUPS):
        ow = jnp.dot(vwT_ref[g, :, pl.ds(ws, wk)], pw[:, gcols(g)], preferred_element_type=F32)
        ow_ref[:, gcols(g)] = ow[0:HEAD_DIM] / jnp.maximum(ow[HEAD_DIM:HEAD_DIM + 1], 1e-30)

    lb = _tile_heads(lb_ref[...])
    s = (jnp.dot(ksa_ref[pl.ds(t0, Q_TILE), 0:LANES], qgT, preferred_element_type=F32)
         + per_group(lambda g: lb))
    m0 = _colmax(s)
    p0 = jnp.exp2(s - m0).astype(BF16)
    for g in range(N_GROUPS):
        acc_ref[g] = jnp.dot(vsT_ref[g, :, pl.ds(t0, Q_TILE)], p0[:, gcols(g)], preferred_element_type=F32)
    ma_ref[...] = m0
    ms_ref[...] = m0

    nq = N_GROUPS * Q_TILE
    blk = lax.broadcasted_iota(jnp.int32, (ns, nq), 0)
    cur = (t0 + (lax.broadcasted_iota(jnp.int32, (ns, nq), 1) & (Q_TILE - 1))) // SEL_LEN
    valid = blk <= cur
    forced = valid & ((blk == 0) | (blk == cur) | (blk == cur - 1))
    rank0 = jnp.where(valid & jnp.logical_not(forced), imp, NEG)
    blk_f = blk.astype(F32)
    taken = -(2.0 ** 126)

    def rank(break_ties):
        score = rank0
        for _ in range(N_SELECT - N_FORCED):
            mx = _colmax(score)
            if break_ties:
                idx = jnp.min(jnp.where(score == mx, blk_f, float(ns)), axis=0, keepdims=True)
                hit = blk_f == idx
            else:
                hit = score == mx
            score = jnp.where(hit, taken, score)
        return (score == taken) & (rank0 > NEG * 0.5)

    picked = rank(False)
    count = jnp.sum(jnp.where(picked, 1.0, 0.0), axis=0, keepdims=True)
    want = jnp.clip(cur[0:1] - (N_FORCED - 1), 0, N_SELECT - N_FORCED).astype(F32)
    sel_ref[...] = jnp.where(forced | picked, 1.0, 0.0)

    @pl.when(jnp.logical_not(jnp.all(count == want)))
    def _():
        sel_ref[...] = jnp.where(forced | rank(True), 1.0, 0.0)

    sel = sel_ref[...] > 0.5
    bias = jnp.where(sel & (blk < b * SEL_PER_Q), 0.0, NEG).astype(BF16)
    for w in range(ns // BIAS_WIN):
        qaug_ref[w, 0:2 * HEAD_DIM, :] = qgT
        qaug_ref[w, 2 * HEAD_DIM:, :] = per_group(
            lambda g: _tile_heads(bias[BIAS_WIN * w:BIAS_WIN * (w + 1), gq(g)]))

    n_tiles = jnp.maximum((t0 + KV_TILE - 1) // KV_TILE, 1)
    tiles_per_win = BIAS_WIN * SEL_LEN // KV_TILE

    def scores(t, slot):
        start = pl.multiple_of(t * KV_TILE, KV_TILE)
        s = jnp.dot(ksa_ref[pl.ds(start, KV_TILE), :], qaug_ref[t // tiles_per_win],
                    preferred_element_type=F32)
        sbuf_ref[slot] = s
        ms_ref[...] = jnp.maximum(ms_ref[...], _colmax(s))

    def consume(t, slot, m_s):
        start = pl.multiple_of(t * KV_TILE, KV_TILE)
        alpha = jnp.exp2(ma_ref[...] - m_s)
        pp = jnp.exp2(sbuf_ref[slot] - m_s).astype(BF16)
        for g in range(N_GROUPS):
            acc_ref[g] = alpha[:, gcols(g)] * acc_ref[g] + jnp.dot(
                vsT_ref[g, :, pl.ds(start, KV_TILE)], pp[:, gcols(g)], preferred_element_type=F32)
        ma_ref[...] = m_s

    def step_(t, slot):
        m_s = ms_ref[...]
        scores(t + 1, 1 - slot)
        consume(t, slot, m_s)

    scores(0, 0)

    def body(i, carry):
        step_(2 * i, 0)
        step_(2 * i + 1, 1)
        return carry

    n_steps = n_tiles - 1
    lax.fori_loop(0, n_steps // 2, body, 0)

    @pl.when(n_steps % 2 == 1)
    def _():
        step_(n_steps - 1, 0)

    consume(n_steps, n_steps % 2, ms_ref[...])

    owT = ow_ref[...]
    osT = per_group(lambda g: acc_ref[g, 0:HEAD_DIM, :] / jnp.maximum(acc_ref[g, HEAD_DIM:HEAD_DIM + 1, :], 1e-30))

    gall = gate_ref[...].T
    row = lambda i: gall[i:i + 1]
    gt = lambda br: jnp.concatenate([row(g * GATE_SLOT + br * GQA + r)
                                     for g in range(N_GROUPS) for r in range(GQA)], axis=1)
    oT = gt(0) * oc_ref[...] + gt(1) * osT + gt(2) * owT
    for j in range(N_HEADS // 2):
        pair = jnp.concatenate([oT[:, Q_TILE * 2 * j:Q_TILE * (2 * j + 1)],
                                oT[:, Q_TILE * (2 * j + 1):Q_TILE * (2 * j + 2)]], axis=0)
        out_ref[:, LANES * j:LANES * (j + 1)] = pair.T


def _attention(q, gates, ksa, vsT, kwa, vwT, kc, vcT, cbt, wb, lb):
    S = q.shape[0]
    nqb = S // Q_TILE
    ncp = kc.shape[1]
    ns = S // SEL_LEN
    assert KV_WIDTH == LANES
    row = lambda b: (b, 0)
    fix = lambda b: (0, 0)
    fix3 = lambda b: (0, 0, 0)
    kernel = functools.partial(_attn_kernel, ns=ns, ncp=ncp)
    return pl.pallas_call(
        kernel,
        grid=(nqb,),
        in_specs=[pl.BlockSpec((Q_TILE, ATTN_WIDTH), row), pl.BlockSpec((Q_TILE, LANES), row),
                  _resident((S, 2 * LANES), fix), _resident((N_GROUPS, V_ROWS, S), fix3),
                  _resident((S, LANES), fix), _resident((N_GROUPS, V_ROWS, S), fix3),
                  _resident((N_GROUPS, ncp, LANES), fix3), _resident((N_GROUPS, HEAD_DIM, ncp), fix3),
                  _resident((2 * ncp, Q_TILE), fix),
                  _resident((WIN_EARLY + 1, WINDOW + Q_TILE, Q_TILE), fix3),
                  _resident((Q_TILE, Q_TILE), fix)],
        out_specs=pl.BlockSpec((Q_TILE, ATTN_WIDTH), row),
        out_shape=jax.ShapeDtypeStruct((S, ATTN_WIDTH), F32),
        scratch_shapes=[pltpu.VMEM((ns // BIAS_WIN, 2 * LANES, N_COLS), BF16),
                        pltpu.VMEM((2, KV_TILE, N_COLS), F32),
                        pltpu.VMEM((N_GROUPS, V_ROWS, ROWS), F32),
                        pltpu.VMEM((1, N_COLS), F32), pltpu.VMEM((1, N_COLS), F32),
                        pltpu.VMEM((N_GROUPS, 8 + ncp, Q_TILE), F32),
                        pltpu.VMEM((HEAD_DIM, N_COLS), F32),
                        pltpu.VMEM((HEAD_DIM, N_COLS), F32),
                        pltpu.VMEM((ns, N_GROUPS * Q_TILE), F32)],
        compiler_params=pltpu.CompilerParams(dimension_semantics=("arbitrary",), vmem_limit_bytes=VMEM_LIMIT),
        name="nsa_attention",
    )(q, gates, ksa, vsT, kwa, vwT, kc, vcT, cbt, wb, lb)


def _gmlp_kernel(u_ref, v_ref, ws_ref, bs_ref, g_ref, out_ref, wsm_ref, *, chunks):
    @pl.when(pl.program_id(0) == 0)
    def _():
        r = lax.broadcasted_iota(jnp.int32, (CHUNK, CHUNK), 0)
        c = lax.broadcasted_iota(jnp.int32, (CHUNK, CHUNK), 1)
        for g in range(MLP_GROUPS):
            wsm_ref[g] = jnp.where(c <= r, ws_ref[g], 0.0).astype(BF16)

    lane = lax.broadcasted_iota(jnp.int32, (CHUNK, LANES), 1)
    left = lane < HEAD_DIM
    for ci in range(chunks):
        rows = slice(ci * CHUNK, (ci + 1) * CHUNK)
        parts = []
        for pr in range(MLP_GROUPS // 2):
            vp = v_ref[rows, pr * LANES:(pr + 1) * LANES]
            va = jnp.where(left, vp, jnp.zeros_like(vp))
            vb = jnp.where(left, jnp.zeros_like(vp), vp)
            parts.append(jnp.dot(wsm_ref[2 * pr], va, preferred_element_type=F32)
                         + jnp.dot(wsm_ref[2 * pr + 1], vb, preferred_element_type=F32))
        mixed = jnp.concatenate(parts, axis=1) + bs_ref[...]
        y = u_ref[rows, :].astype(F32) * mixed
        out_ref[rows, :] = _rms(y, g_ref[...]).astype(BF16)


def _gmlp(u, v, ws, bs_exp, gain, chunks=4):
    S = u.shape[0]
    tm = CHUNK * chunks
    row = lambda i: (i, 0)
    return pl.pallas_call(
        functools.partial(_gmlp_kernel, chunks=chunks),
        grid=(S // tm,),
        in_specs=[pl.BlockSpec((tm, MLP_WIDTH), row), pl.BlockSpec((tm, MLP_WIDTH), row),
                  _resident((MLP_GROUPS, CHUNK, CHUNK), lambda i: (0, 0, 0)),
                  _resident((CHUNK, MLP_WIDTH), lambda i: (0, 0)),
                  _resident((1, MLP_WIDTH), lambda i: (0, 0))],
        out_specs=pl.BlockSpec((tm, MLP_WIDTH), row),
        out_shape=jax.ShapeDtypeStruct((S, MLP_WIDTH), BF16),
        scratch_shapes=[pltpu.VMEM((MLP_GROUPS, CHUNK, CHUNK), BF16)],
        compiler_params=pltpu.CompilerParams(dimension_semantics=("arbitrary",), vmem_limit_bytes=VMEM_LIMIT),
        name="gmlp",
    )(u, v, ws, bs_exp, gain)


def _outproj_kernel(h_ref, a_ref, m_ref, ag_ref, wa_ref, wm_ref, pg_ref, out_ref):
    an = _rms(a_ref[...], ag_ref[...]).astype(BF16)
    mix = (jnp.dot(an, wa_ref[...], preferred_element_type=F32)
           + jnp.dot(m_ref[...], wm_ref[...], preferred_element_type=F32))
    out_ref[...] = h_ref[...] + _rms(mix, pg_ref[...])


def _outproj(h, attn, mlpn, ag, wa, wm, pg, tm=512):
    S, D = h.shape
    row = lambda i: (i, 0)
    fix = lambda i: (0, 0)
    return pl.pallas_call(
        _outproj_kernel,
        grid=(S // tm,),
        in_specs=[pl.BlockSpec((tm, D), row), pl.BlockSpec((tm, ATTN_WIDTH), row), pl.BlockSpec((tm, MLP_WIDTH), row),
                  _resident((1, ATTN_WIDTH), fix), _resident((ATTN_WIDTH, D), fix),
                  _resident((MLP_WIDTH, D), fix), _resident((1, D), fix)],
        out_specs=pl.BlockSpec((tm, D), row),
        out_shape=jax.ShapeDtypeStruct((S, D), F32),
        compiler_params=pltpu.CompilerParams(dimension_semantics=("arbitrary",), vmem_limit_bytes=VMEM_LIMIT),
        name="outproj",
    )(h, attn, mlpn, ag, wa, wm, pg)


HALO = BF16_ROWS


def _ffn_kernel(h_ref, halo_ref, p_ref, g_ref, wup_ref, cw_ref, cb_ref, wd_ref, pg_ref, plg_ref, wpg_ref, wpp_ref,
                out_ref, acc_ref, *, fc):
    dff = wd_ref.shape[0]
    h = h_ref[...]
    hn = _rms(halo_ref[...], g_ref[...])
    hn = jnp.where(pl.program_id(0) == 0, 0.0, hn)
    xn = jnp.concatenate([hn.astype(BF16), _rms(h, g_ref[...]).astype(BF16)], axis=0)

    def conv(lo, n):
        hh = jnp.dot(xn, wup_ref[:, lo:lo + n], preferred_element_type=F32)
        cw = cw_ref[:, lo:lo + n]
        y = cb_ref[:, lo:lo + n] + pltpu.roll(hh, 2, 0) * cw[0:1] + pltpu.roll(hh, 1, 0) * cw[1:2] + hh * cw[2:3]
        return y[HALO:, :]

    for c0 in range(0, dff, fc):
        n = min(fc, dff - c0)
        act = (jax.nn.silu(conv(c0, n)) * conv(dff + c0, n)).astype(BF16)
        part = jnp.dot(act, wd_ref[c0:c0 + n, :], preferred_element_type=F32)
        if c0 == 0:
            acc_ref[...] = part
        else:
            acc_ref[...] += part

    h2 = h + _rms(acc_ref[...], pg_ref[...])
    gate = jax.nn.sigmoid(jnp.dot(_rms(h2, plg_ref[...]).astype(BF16), wpg_ref[...], preferred_element_type=F32))
    proj = jnp.dot(p_ref[...].astype(BF16), wpp_ref[...], preferred_element_type=F32)
    out_ref[...] = h2 + gate * proj


def _ffn(h, p, gain, w_up, conv_w, conv_b, w_down, pg, plg, wpg, wpp, tm=512, fc=1024):
    S, D = h.shape
    dff = w_down.shape[0]
    P = p.shape[1]
    row = lambda i: (i, 0)
    fix = lambda i: (0, 0)
    halo = lambda i: (jnp.maximum(i * (tm // HALO) - 1, 0), 0)
    return pl.pallas_call(
        functools.partial(_ffn_kernel, fc=fc),
        grid=(S // tm,),
        in_specs=[pl.BlockSpec((tm, D), row), pl.BlockSpec((HALO, D), halo), pl.BlockSpec((tm, P), row),
                  _resident((1, D), fix), _resident((D, 2 * dff), fix), _resident((CONV_WIDTH, 2 * dff), fix),
                  _resident((1, 2 * dff), fix), _resident((dff, D), fix), _resident((1, D), fix),
                  _resident((1, D), fix), _resident((D, D), fix), _resident((P, D), fix)],
        out_specs=pl.BlockSpec((tm, D), row),
        out_shape=jax.ShapeDtypeStruct((S, D), F32),
        scratch_shapes=[pltpu.VMEM((tm, D), F32)],
        compiler_params=pltpu.CompilerParams(dimension_semantics=("arbitrary",), vmem_limit_bytes=VMEM_LIMIT),
        name="convffn_ple",
    )(h, h, p, gain, w_up, conv_w, conv_b, w_down, pg, plg, wpg, wpp)


def _rope_tables(pos):
    half = ROT_DIM // 2
    inv = ROPE_THETA ** (-np.arange(half, dtype=np.float64) / half)
    ang = np.asarray(pos, np.float64)[:, None] * inv[None, :]
    cos, sin = jnp.asarray(np.cos(ang), dtype=F32), jnp.asarray(np.sin(ang), dtype=F32)
    n = ang.shape[0]
    one = jnp.ones((n, HEAD_DIM - ROT_DIM), F32)
    zero = jnp.zeros((n, HEAD_DIM - ROT_DIM), F32)
    zh = jnp.zeros((n, half), F32)
    c = jnp.concatenate([cos, cos, one], axis=1)
    sm = jnp.concatenate([-sin, zh, zero], axis=1)
    sp = jnp.concatenate([zh, sin, zero], axis=1)
    dup = lambda t: jnp.concatenate([t, t], axis=1)
    return dup(c), dup(sm), dup(sp)


def _mask_tables(ncp):
    qo = np.arange(Q_TILE)[None, :]
    rel = (np.arange(2 * ncp) - ncp)[:, None]
    cmp_ok = rel * CMP_STRIDE + CMP_LEN - 1 <= qo
    r = np.arange(WINDOW + Q_TILE)[:, None]
    win_ok = []
    for v in range(WIN_EARLY):
        tq = v * Q_TILE + qo
        win_ok.append((r <= tq) & (r > tq - WINDOW))
    win_ok.append((r - WINDOW <= qo) & (r > qo))
    loc_ok = np.arange(Q_TILE)[:, None] <= qo
    tab = lambda ok: jnp.asarray(np.where(ok, 0.0, NEG), dtype=F32)
    return tab(cmp_ok), tab(np.stack(win_ok)), tab(loc_ok)


def _gate_columns():
    src = np.full((LANES,), -1)
    for g in range(N_GROUPS):
        for br in range(N_BRANCH):
            for r in range(GQA):
                src[g * GATE_SLOT + br * GQA + r] = (g * GQA + r) * N_BRANCH + br
    return src


def _layer(h, p, prm, tabs):
    S, D = h.shape
    rc, rm, rp, crc, crm, crp, cbt, wb, lb = tabs

    sizes = [ATTN_WIDTH] + [KV_WIDTH] * 6 + [N_HEADS * N_BRANCH, MLP_WIDTH, MLP_WIDTH]
    offs = np.concatenate([[0], np.cumsum(sizes)])
    w_in = prm["w_in"]
    seg = lambda k: w_in[:, offs[k]:offs[k + 1]]
    src = _gate_columns()
    wg = jnp.where(jnp.asarray(src >= 0)[None, :], seg(7)[:, np.maximum(src, 0)], 0.0)
    w_cat = jnp.concatenate([seg(0)] + [seg(k) for k in range(1, 7)] + [seg(8), seg(9), wg], axis=1).astype(BF16)

    q, ksa, kwa, vsT, vwT, kcx, vcx, u, v, gates = _inproj(h, prm["pre_mix_g"][None], w_cat, rc, rm, rp,
                                                           prm["gmlp_ln_g"][None], prm["gmlp_ln_b"][None])

    w1 = prm["cmp_w1"].astype(BF16)
    half = CMP_STRIDE * HEAD_DIM
    w1l = jnp.concatenate([w1[:, :half], w1[:, half:]], axis=2).reshape(2, CMP_STRIDE, HEAD_DIM, 2 * CMP_HIDDEN)
    zl = jnp.zeros_like(w1l)
    wboth = jnp.concatenate([jnp.concatenate([w1l, zl], axis=3), jnp.concatenate([zl, w1l], axis=3)], axis=2)
    wboth = wboth.reshape(2, CMP_STRIDE * LANES, 2 * N_GROUPS * CMP_HIDDEN)
    pe = jnp.broadcast_to(prm["cmp_pe"].reshape(2, 1, CMP_LEN * HEAD_DIM), (2, 8, CMP_LEN * HEAD_DIM)).astype(BF16)
    w2 = jnp.pad(prm["cmp_w2"], ((0, 0), (0, 0), (0, LANES - HEAD_DIM))).astype(BF16)
    b2 = jnp.pad(prm["cmp_b2"], ((0, 0), (0, LANES - HEAD_DIM)))[:, None]
    kc, vcT = _compress(kcx, vcx, wboth, w1, pe, prm["cmp_b1"][:, None], w2, b2, crc, crm, crp)

    attn = _attention(q, gates, ksa, vsT, kwa, vwT, kc, vcT, cbt, wb, lb)

    bs_exp = jnp.repeat(prm["gmlp_bs"].T, HEAD_DIM, axis=1)
    mlpn = _gmlp(u, v, prm["gmlp_ws"], bs_exp, prm["mlp_out_g"][None])

    w_o = prm["w_o"].astype(BF16)
    h = _outproj(h, attn, mlpn, prm["attn_out_g"][None], w_o[:ATTN_WIDTH], w_o[ATTN_WIDTH:], prm["post_mix_g"][None])
    h = _ffn(h, p, prm["pre_ffn_g"][None], prm["w_up"].astype(BF16), prm["conv_w"], prm["conv_b"][None],
             prm["w_down"].astype(BF16), prm["post_ffn_g"][None], prm["ple_norm_g"][None],
             prm["w_ple_gate"].astype(BF16), prm["w_ple_proj"].astype(BF16))
    return h


def kernel(x, p, pre_mix_g, w_in, cmp_pe, cmp_w1, cmp_b1, cmp_w2, cmp_b2, gmlp_ln_g, gmlp_ln_b, gmlp_ws, gmlp_bs,
           attn_out_g, mlp_out_g, w_o, post_mix_g, pre_ffn_g, w_up, conv_w, conv_b, w_down, post_ffn_g,
           ple_norm_g, w_ple_gate, w_ple_proj):
    B, S, D = x.shape
    depth = p.shape[0]
    assert S % (BIAS_WIN * SEL_LEN) == 0 and D % LANES == 0
    nch = S // CMP_STRIDE
    rc, rm, rp = _rope_tables(np.arange(S))
    crc, crm, crp = _rope_tables(np.arange(nch) * CMP_STRIDE + CMP_LEN - 1)
    tabs = (rc, rm, rp, crc, crm, crp) + _mask_tables(nch)
    stacked = dict(pre_mix_g=pre_mix_g, w_in=w_in, cmp_pe=cmp_pe, cmp_w1=cmp_w1, cmp_b1=cmp_b1, cmp_w2=cmp_w2,
                   cmp_b2=cmp_b2, gmlp_ln_g=gmlp_ln_g, gmlp_ln_b=gmlp_ln_b, gmlp_ws=gmlp_ws, gmlp_bs=gmlp_bs,
                   attn_out_g=attn_out_g, mlp_out_g=mlp_out_g, w_o=w_o, post_mix_g=post_mix_g, pre_ffn_g=pre_ffn_g,
                   w_up=w_up, conv_w=conv_w, conv_b=conv_b, w_down=w_down, post_ffn_g=post_ffn_g,
                   ple_norm_g=ple_norm_g, w_ple_gate=w_ple_gate, w_ple_proj=w_ple_proj)
    outs = []
    for bi in range(B):
        h = x[bi]
        for i in range(depth):
            h = _layer(h, p[i, bi], {k: v[i] for k, v in stacked.items()}, tabs)
        outs.append(h)
    return jnp.stack(outs)
```

```python
import functools
import math

import numpy as np
import jax
import jax.numpy as jnp
from jax import lax
from jax.experimental import pallas as pl
from jax.experimental.pallas import tpu as pltpu

F32 = jnp.float32
BF16 = jnp.bfloat16

HEAD_DIM = 64
N_HEADS = 8
N_GROUPS = 2
GQA = N_HEADS // N_GROUPS
N_BRANCH = 3
ATTN_WIDTH = N_HEADS * HEAD_DIM
MLP_GROUPS = 8
MLP_WIDTH = MLP_GROUPS * HEAD_DIM
KV_WIDTH = N_GROUPS * HEAD_DIM
ROT_DIM = HEAD_DIM // 4
ROPE_THETA = 500000.0
CMP_LEN = 32
CMP_STRIDE = 16
CMP_HIDDEN = 256
SEL_LEN = 64
N_SELECT = 16
WINDOW = 512
CHUNK = 128
CONV_WIDTH = 3
NORM_EPS = 1e-6
NEG = -1e30
SCALE = HEAD_DIM ** -0.5
QSCALE = SCALE * math.log2(math.e)

LANES = 128
BF16_ROWS = 16
Q_TILE = 128
ROWS = GQA * Q_TILE
N_COLS = N_GROUPS * ROWS
KV_TILE = 1024
BIAS_WIN = 128
V_ROWS = HEAD_DIM + BF16_ROWS
CMP_PER_Q = Q_TILE // CMP_STRIDE
SEL_PER_Q = Q_TILE // SEL_LEN
CMP_CLASSES = 8
N_FORCED = 3
GATE_SLOT = 16
WIN_EARLY = WINDOW // Q_TILE
IMP_TAPS = range(-1, SEL_LEN // CMP_STRIDE)
VMEM_LIMIT = 56 * 1024 * 1024


def _rms(x, gain):
    return x * lax.rsqrt(jnp.mean(x * x, axis=-1, keepdims=True) + NORM_EPS) * gain


def _rope_slab(x, c, sm, sp):
    return x * c + pltpu.roll(x, LANES - ROT_DIM // 2, 1) * sm + pltpu.roll(x, ROT_DIM // 2, 1) * sp


def _colmax(s):
    return jnp.max(s, axis=0, keepdims=True)


def _tile_heads(x):
    return jnp.concatenate([x] * GQA, axis=1)


def _resident(shape, index_map):
    return pl.BlockSpec(shape, index_map, pipeline_mode=pl.Buffered(1))


def _inproj_kernel(x_ref, g_ref, w_ref, rc_ref, rm_ref, rp_ref, lng_ref, lnb_ref,
                   q_ref, ksa_ref, kwa_ref, vsT_ref, vwT_ref, kcx_ref, vcx_ref, u_ref, v_ref, gate_ref, craw_ref):
    tm = x_ref.shape[0]
    a = _rms(x_ref[...], g_ref[...]).astype(BF16)
    c, sm, sp = rc_ref[...], rm_ref[...], rp_ref[...]
    zq = jnp.dot(a, w_ref[:, 0:ATTN_WIDTH], preferred_element_type=F32)
    for j in range(ATTN_WIDTH // LANES):
        sl = slice(LANES * j, LANES * (j + 1))
        q_ref[:, sl] = (_rope_slab(zq[:, sl], c, sm, sp) * QSCALE).astype(BF16)
    o = ATTN_WIDTH
    zkv = jnp.dot(a, w_ref[:, o:o + 6 * KV_WIDTH], preferred_element_type=F32)
    slab = lambda j: zkv[:, LANES * j:LANES * (j + 1)]

    craw_ref[0] = slab(0)
    craw_ref[1] = slab(1)
    nch = tm // CMP_STRIDE
    for l in range(CMP_STRIDE):
        kcx_ref[l] = craw_ref[0, pl.ds(l, nch, stride=CMP_STRIDE), :].astype(BF16)
        vcx_ref[l] = craw_ref[1, pl.ds(l, nch, stride=CMP_STRIDE), :].astype(BF16)

    pos = pl.program_id(0) * tm + lax.broadcasted_iota(jnp.int32, (tm, BIAS_WIN), 0)
    lane = lax.broadcasted_iota(jnp.int32, (tm, BIAS_WIN), 1)
    ksa_ref[:, 0:LANES] = _rope_slab(slab(2), c, sm, sp).astype(BF16)
    ksa_ref[:, LANES:] = jnp.where((pos // SEL_LEN) % BIAS_WIN == lane, 1.0, 0.0).astype(BF16)
    kwa_ref[...] = _rope_slab(slab(4), c, sm, sp).astype(BF16)
    for k in range(tm // LANES):
        rs = slice(LANES * k, LANES * (k + 1))
        for ref, sl in ((vsT_ref, slab(3)), (vwT_ref, slab(5))):
            t = sl[rs, :].T.astype(BF16)
            for g in range(N_GROUPS):
                ref[g, 0:HEAD_DIM, rs] = t[HEAD_DIM * g:HEAD_DIM * (g + 1)]
                ref[g, HEAD_DIM:, rs] = jnp.ones((BF16_ROWS, LANES), BF16)

    o += 6 * KV_WIDTH
    zu = jnp.dot(a, w_ref[:, o:o + MLP_WIDTH], preferred_element_type=F32)
    u_ref[...] = jax.nn.gelu(zu).astype(BF16)
    o += MLP_WIDTH
    zv = jax.nn.gelu(jnp.dot(a, w_ref[:, o:o + MLP_WIDTH], preferred_element_type=F32))
    mu = jnp.mean(zv, axis=-1, keepdims=True)
    d = zv - mu
    var = jnp.mean(d * d, axis=-1, keepdims=True)
    v_ref[...] = (d * lax.rsqrt(var + NORM_EPS) * lng_ref[...] + lnb_ref[...]).astype(BF16)
    o += MLP_WIDTH
    zg = jnp.dot(a, w_ref[:, o:o + LANES], preferred_element_type=F32)
    gate_ref[...] = jax.nn.sigmoid(zg)


def _inproj(h, gain, w, rc, rm, rp, lng, lnb, tm=512):
    S, D = h.shape
    N = w.shape[1]
    nch = S // CMP_STRIDE
    row = lambda i: (i, 0)
    fix = lambda i: (0, 0)
    chunked = lambda i: (0, i, 0)
    bf = lambda *shape: jax.ShapeDtypeStruct(shape, BF16)
    return pl.pallas_call(
        _inproj_kernel,
        grid=(S // tm,),
        in_specs=[pl.BlockSpec((tm, D), row), _resident((1, D), fix), _resident((D, N), fix),
                  pl.BlockSpec((tm, LANES), row), pl.BlockSpec((tm, LANES), row), pl.BlockSpec((tm, LANES), row),
                  _resident((1, MLP_WIDTH), fix), _resident((1, MLP_WIDTH), fix)],
        out_specs=[pl.BlockSpec((tm, ATTN_WIDTH), row), pl.BlockSpec((tm, 2 * LANES), row),
                   pl.BlockSpec((tm, LANES), row), pl.BlockSpec((N_GROUPS, V_ROWS, tm), lambda i: (0, 0, i)),
                   pl.BlockSpec((N_GROUPS, V_ROWS, tm), lambda i: (0, 0, i)),
                   pl.BlockSpec((CMP_STRIDE, tm // CMP_STRIDE, LANES), chunked),
                   pl.BlockSpec((CMP_STRIDE, tm // CMP_STRIDE, LANES), chunked),
                   pl.BlockSpec((tm, MLP_WIDTH), row), pl.BlockSpec((tm, MLP_WIDTH), row),
                   pl.BlockSpec((tm, LANES), row)],
        out_shape=[bf(S, ATTN_WIDTH), bf(S, 2 * LANES), bf(S, LANES), bf(N_GROUPS, V_ROWS, S), bf(N_GROUPS, V_ROWS, S),
                   bf(CMP_STRIDE, nch, LANES), bf(CMP_STRIDE, nch, LANES), bf(S, MLP_WIDTH), bf(S, MLP_WIDTH),
                   jax.ShapeDtypeStruct((S, LANES), F32)],
        scratch_shapes=[pltpu.VMEM((2, tm, LANES), F32)],
        compiler_params=pltpu.CompilerParams(dimension_semantics=("arbitrary",), vmem_limit_bytes=VMEM_LIMIT),
        name="inproj",
    )(h, gain, w, rc, rm, rp, lng, lnb)


def _compress_kernel(xk_ref, xv_ref, wb_ref, w1_ref, pe_ref, b1_ref, w2_ref, b2_ref, rc_ref, rm_ref, rp_ref,
                     kc_ref, vcT_ref):
    nch = xk_ref.shape[1]

    def run(x_ref):
        x = jnp.concatenate([x_ref[l] for l in range(CMP_STRIDE)], axis=1)
        c = jnp.dot(x, wb_ref[0], preferred_element_type=F32)
        cvec = jnp.dot(pe_ref[0], w1_ref[0], preferred_element_type=F32)[0:1] + b1_ref[0]
        outs = []
        for g in range(N_GROUPS):
            cg = c[:, 2 * CMP_HIDDEN * g:2 * CMP_HIDDEN * (g + 1)]
            h = jax.nn.gelu(cg[:, :CMP_HIDDEN] + pltpu.roll(cg[:, CMP_HIDDEN:], nch - 1, 0) + cvec)
            outs.append(jnp.dot(h.astype(BF16), w2_ref[0], preferred_element_type=F32) + b2_ref[0])
        return outs

    @pl.when(pl.program_id(0) == 0)
    def _():
        o = run(xk_ref)
        shared = o[0]
        for g in range(1, N_GROUPS):
            shared = shared + pltpu.roll(o[g], HEAD_DIM * g, 1)
        kc_ref[...] = _rope_slab(shared, rc_ref[...], rm_ref[...], rp_ref[...]).astype(BF16)

    @pl.when(pl.program_id(0) == 1)
    def _():
        for g, o in enumerate(run(xv_ref)):
            for k in range(nch // LANES):
                rs = slice(LANES * k, LANES * (k + 1))
                vcT_ref[g, :, rs] = o[rs, :].T[0:HEAD_DIM].astype(BF16)


def _compress(kcx, vcx, wboth, w1, pe, b1, w2, b2, rc, rm, rp):
    _, nch, _ = kcx.shape
    cw = CMP_STRIDE * HEAD_DIM
    kv3 = lambda k: (k, 0, 0)
    fix3 = lambda k: (0, 0, 0)
    fix2 = lambda k: (0, 0)
    return pl.pallas_call(
        _compress_kernel,
        grid=(2,),
        in_specs=[_resident((CMP_STRIDE, nch, LANES), fix3), _resident((CMP_STRIDE, nch, LANES), fix3),
                  pl.BlockSpec((1, CMP_STRIDE * LANES, 2 * N_GROUPS * CMP_HIDDEN), kv3),
                  pl.BlockSpec((1, 2 * cw, CMP_HIDDEN), kv3), pl.BlockSpec((1, 8, 2 * cw), kv3),
                  pl.BlockSpec((1, 1, CMP_HIDDEN), kv3), pl.BlockSpec((1, CMP_HIDDEN, LANES), kv3),
                  pl.BlockSpec((1, 1, LANES), kv3),
                  _resident((nch, LANES), fix2), _resident((nch, LANES), fix2), _resident((nch, LANES), fix2)],
        out_specs=[pl.BlockSpec((nch, LANES), fix2), pl.BlockSpec((N_GROUPS, HEAD_DIM, nch), fix3)],
        out_shape=[jax.ShapeDtypeStruct((nch, LANES), BF16),
                   jax.ShapeDtypeStruct((N_GROUPS, HEAD_DIM, nch), BF16)],
        compiler_params=pltpu.CompilerParams(dimension_semantics=("arbitrary",), vmem_limit_bytes=VMEM_LIMIT),
        name="compress",
    )(kcx, vcx, wboth, w1, pe, b1, w2, b2, rc, rm, rp)


def _attn_kernel(q_ref, gate_ref, ksa_ref, vsT_ref, kwa_ref, vwT_ref, kc_ref, vcT_ref, cbt_ref, wb_ref, lb_ref,
                 out_ref, qaug_ref, sbuf_ref, acc_ref, ma_ref, ms_ref, ps_ref, oc_ref, ow_ref, sel_ref,
                 *, ns, ncp):
    b = pl.program_id(0)
    t0 = pl.multiple_of(b * Q_TILE, Q_TILE)
    gcols = lambda g: slice(ROWS * g, ROWS * (g + 1))
    gq = lambda g: slice(Q_TILE * g, Q_TILE * (g + 1))
    per_group = lambda f: jnp.concatenate([f(g) for g in range(N_GROUPS)], axis=1)

    qf = q_ref[...].astype(F32)
    parts = []
    for j in range(N_HEADS // 2):
        t = qf[:, LANES * j:LANES * (j + 1)].T
        parts += [t[0:HEAD_DIM], t[HEAD_DIM:]]
    qT = jnp.concatenate(parts, axis=1).astype(BF16)
    zeros = jnp.zeros((HEAD_DIM, ROWS), BF16)
    qgT = jnp.concatenate(
        [jnp.concatenate([qT[:, gcols(g)] if gg == g else zeros for gg in range(N_GROUPS)], axis=1)
         for g in range(N_GROUPS)], axis=0)

    step = ncp // CMP_CLASSES
    cls = (b * CMP_PER_Q + CMP_PER_Q - 2) // step
    for k in range(CMP_CLASSES):
        @pl.when(cls == k)
        def _(nk=(k + 1) * step):
            cb = _tile_heads(cbt_ref[pl.ds(pl.multiple_of(ncp - b * CMP_PER_Q, 8), nk), :])
            s = jnp.dot(kc_ref[0:nk, :], qgT, preferred_element_type=F32) + per_group(lambda g: cb)
            m = _colmax(s)
            e = jnp.exp2(s - m)
            inv = jnp.where(m > NEG * 0.5, 1.0 / jnp.maximum(jnp.sum(e, axis=0, keepdims=True), 1e-30), 0.0)
            p = e * inv
            for g in range(N_GROUPS):
                pg = p[:, gcols(g)]
                oc_ref[:, gcols(g)] = jnp.dot(vcT_ref[g, :, 0:nk], pg.astype(BF16), preferred_element_type=F32)
                ps = pg[:, 0:Q_TILE]
                for r in range(1, GQA):
                    ps = ps + pg[:, r * Q_TILE:(r + 1) * Q_TILE]
                ps_ref[g, 0:8, :] = jnp.zeros((8, Q_TILE), F32)
                ps_ref[g, 8:8 + nk, :] = ps
                if nk < ncp:
                    ps_ref[g, 8 + nk:, :] = jnp.zeros((ncp - nk, Q_TILE), F32)

    ratio = SEL_LEN // CMP_STRIDE
    imp = per_group(lambda g: sum(ps_ref[g, pl.ds(8 + d, ns, stride=ratio), :] for d in IMP_TAPS))

    wk = WINDOW + Q_TILE
    ws = pl.multiple_of(jnp.maximum(t0 - WINDOW, 0), Q_TILE)
    wb = _tile_heads(wb_ref[jnp.minimum(b, WIN_EARLY)])
    s = jnp.dot(kwa_ref[pl.ds(ws, wk), :], qgT, preferred_element_type=F32) + per_group(lambda g: wb)
    pw = jnp.exp2(s - _colmax(s)).astype(BF16)
    for g in range(N_GROUPS):
        ow = jnp.dot(vwT_ref[g, :, pl.ds(ws, wk)], pw[:, gcols(g)], preferred_element_type=F32)
        ow_ref[:, gcols(g)] = ow[0:HEAD_DIM] / jnp.maximum(ow[HEAD_DIM:HEAD_DIM + 1], 1e-30)

    lb = _tile_heads(lb_ref[...])
    s = (jnp.dot(ksa_ref[pl.ds(t0, Q_TILE), 0:LANES], qgT, preferred_element_type=F32)
         + per_group(lambda g: lb))
    m0 = _colmax(s)
    p0 = jnp.exp2(s - m0).astype(BF16)
    for g in range(N_GROUPS):
        acc_ref[g] = jnp.dot(vsT_ref[g, :, pl.ds(t0, Q_TILE)], p0[:, gcols(g)], preferred_element_type=F32)
    ma_ref[...] = m0
    ms_ref[...] = m0

    nq = N_GROUPS * Q_TILE
    blk = lax.broadcasted_iota(jnp.int32, (ns, nq), 0)
    cur = (t0 + (lax.broadcasted_iota(jnp.int32, (ns, nq), 1) & (Q_TILE - 1))) // SEL_LEN
    valid = blk <= cur
    forced = valid & ((blk == 0) | (blk == cur) | (blk == cur - 1))
    rank0 = jnp.where(valid & jnp.logical_not(forced), imp, NEG)
    blk_f = blk.astype(F32)
    taken = -(2.0 ** 126)

    def rank(break_ties):
        score = rank0
        for _ in range(N_SELECT - N_FORCED):
            mx = _colmax(score)
            if break_ties:
                idx = jnp.min(jnp.where(score == mx, blk_f, float(ns)), axis=0, keepdims=True)
                hit = blk_f == idx
            else:
                hit = score == mx
            score = jnp.where(hit, taken, score)
        return (score == taken) & (rank0 > NEG * 0.5)

    picked = rank(False)
    count = jnp.sum(jnp.where(picked, 1.0, 0.0), axis=0, keepdims=True)
    want = jnp.clip(cur[0:1] - (N_FORCED - 1), 0, N_SELECT - N_FORCED).astype(F32)
    sel_ref[...] = jnp.where(forced | picked, 1.0, 0.0)

    @pl.when(jnp.logical_not(jnp.all(count == want)))
    def _():
        sel_ref[...] = jnp.where(forced | rank(True), 1.0, 0.0)

    sel = sel_ref[...] > 0.5
    bias = jnp.where(sel & (blk < b * SEL_PER_Q), 0.0, NEG).astype(BF16)
    for w in range(ns // BIAS_WIN):
        qaug_ref[w, 0:2 * HEAD_DIM, :] = qgT
        qaug_ref[w, 2 * HEAD_DIM:, :] = per_group(
            lambda g: _tile_heads(bias[BIAS_WIN * w:BIAS_WIN * (w + 1), gq(g)]))

    n_tiles = jnp.maximum((t0 + KV_TILE - 1) // KV_TILE, 1)
    tiles_per_win = BIAS_WIN * SEL_LEN // KV_TILE

    def scores(t, slot):
        start = pl.multiple_of(t * KV_TILE, KV_TILE)
        s = jnp.dot(ksa_ref[pl.ds(start, KV_TILE), :], qaug_ref[t // tiles_per_win],
                    preferred_element_type=F32)
        sbuf_ref[slot] = s
        ms_ref[...] = jnp.maximum(ms_ref[...], _colmax(s))

    def consume(t, slot, m_s):
        start = pl.multiple_of(t * KV_TILE, KV_TILE)
        alpha = jnp.exp2(ma_ref[...] - m_s)
        pp = jnp.exp2(sbuf_ref[slot] - m_s).astype(BF16)
        for g in range(N_GROUPS):
            acc_ref[g] = alpha[:, gcols(g)] * acc_ref[g] + jnp.dot(
                vsT_ref[g, :, pl.ds(start, KV_TILE)], pp[:, gcols(g)], preferred_element_type=F32)
        ma_ref[...] = m_s

    def step_(t, slot):
        m_s = ms_ref[...]
        scores(t + 1, 1 - slot)
        consume(t, slot, m_s)

    scores(0, 0)

    def body(i, carry):
        step_(2 * i, 0)
        step_(2 * i + 1, 1)
        return carry

    n_steps = n_tiles - 1
    lax.fori_loop(0, n_steps // 2, body, 0)

    @pl.when(n_steps % 2 == 1)
    def _():
        step_(n_steps - 1, 0)

    consume(n_steps, n_steps % 2, ms_ref[...])

    owT = ow_ref[...]
    osT = per_group(lambda g: acc_ref[g, 0:HEAD_DIM, :] / jnp.maximum(acc_ref[g, HEAD_DIM:HEAD_DIM + 1, :], 1e-30))

    gall = gate_ref[...].T
    row = lambda i: gall[i:i + 1]
    gt = lambda br: jnp.concatenate([row(g * GATE_SLOT + br * GQA + r)
                                     for g in range(N_GROUPS) for r in range(GQA)], axis=1)
    oT = gt(0) * oc_ref[...] + gt(1) * osT + gt(2) * owT
    for j in range(N_HEADS // 2):
        pair = jnp.concatenate([oT[:, Q_TILE * 2 * j:Q_TILE * (2 * j + 1)],
                                oT[:, Q_TILE * (2 * j + 1):Q_TILE * (2 * j + 2)]], axis=0)
        out_ref[:, LANES * j:LANES * (j + 1)] = pair.T


def _attention(q, gates, ksa, vsT, kwa, vwT, kc, vcT, cbt, wb, lb):
    S = q.shape[0]
    nqb = S // Q_TILE
    ncp = kc.shape[0]
    ns = S // SEL_LEN
    assert KV_WIDTH == LANES
    row = lambda b: (b, 0)
    fix = lambda b: (0, 0)
    fix3 = lambda b: (0, 0, 0)
    kernel = functools.partial(_attn_kernel, ns=ns, ncp=ncp)
    return pl.pallas_call(
        kernel,
        grid=(nqb,),
        in_specs=[pl.BlockSpec((Q_TILE, ATTN_WIDTH), row), pl.BlockSpec((Q_TILE, LANES), row),
                  _resident((S, 2 * LANES), fix), _resident((N_GROUPS, V_ROWS, S), fix3),
                  _resident((S, LANES), fix), _resident((N_GROUPS, V_ROWS, S), fix3),
                  _resident((ncp, LANES), fix), _resident((N_GROUPS, HEAD_DIM, ncp), fix3),
                  _resident((2 * ncp, Q_TILE), fix),
                  _resident((WIN_EARLY + 1, WINDOW + Q_TILE, Q_TILE), fix3),
                  _resident((Q_TILE, Q_TILE), fix)],
        out_specs=pl.BlockSpec((Q_TILE, ATTN_WIDTH), row),
        out_shape=jax.ShapeDtypeStruct((S, ATTN_WIDTH), F32),
        scratch_shapes=[pltpu.VMEM((ns // BIAS_WIN, 2 * LANES, N_COLS), BF16),
                        pltpu.VMEM((2, KV_TILE, N_COLS), F32),
                        pltpu.VMEM((N_GROUPS, V_ROWS, ROWS), F32),
                        pltpu.VMEM((1, N_COLS), F32), pltpu.VMEM((1, N_COLS), F32),
                        pltpu.VMEM((N_GROUPS, 8 + ncp, Q_TILE), F32),
                        pltpu.VMEM((HEAD_DIM, N_COLS), F32),
                        pltpu.VMEM((HEAD_DIM, N_COLS), F32),
                        pltpu.VMEM((ns, N_GROUPS * Q_TILE), F32)],
        compiler_params=pltpu.CompilerParams(dimension_semantics=("arbitrary",), vmem_limit_bytes=VMEM_LIMIT),
        name="nsa_attention",
    )(q, gates, ksa, vsT, kwa, vwT, kc, vcT, cbt, wb, lb)


def _gmlp_kernel(u_ref, v_ref, ws_ref, bs_ref, g_ref, out_ref, wsm_ref, *, chunks):
    @pl.when(pl.program_id(0) == 0)
    def _():
        r = lax.broadcasted_iota(jnp.int32, (CHUNK, CHUNK), 0)
        c = lax.broadcasted_iota(jnp.int32, (CHUNK, CHUNK), 1)
        for g in range(MLP_GROUPS):
            wsm_ref[g] = jnp.where(c <= r, ws_ref[g], 0.0).astype(BF16)

    lane = lax.broadcasted_iota(jnp.int32, (CHUNK, LANES), 1)
    left = lane < HEAD_DIM
    for ci in range(chunks):
        rows = slice(ci * CHUNK, (ci + 1) * CHUNK)
        parts = []
        for pr in range(MLP_GROUPS // 2):
            vp = v_ref[rows, pr * LANES:(pr + 1) * LANES]
            va = jnp.where(left, vp, jnp.zeros_like(vp))
            vb = jnp.where(left, jnp.zeros_like(vp), vp)
            parts.append(jnp.dot(wsm_ref[2 * pr], va, preferred_element_type=F32)
                         + jnp.dot(wsm_ref[2 * pr + 1], vb, preferred_element_type=F32))
        mixed = jnp.concatenate(parts, axis=1) + bs_ref[...]
        y = u_ref[rows, :].astype(F32) * mixed
        out_ref[rows, :] = _rms(y, g_ref[...]).astype(BF16)


def _gmlp(u, v, ws, bs_exp, gain, chunks=8):
    S = u.shape[0]
    tm = CHUNK * chunks
    row = lambda i: (i, 0)
    return pl.pallas_call(
        functools.partial(_gmlp_kernel, chunks=chunks),
        grid=(S // tm,),
        in_specs=[pl.BlockSpec((tm, MLP_WIDTH), row), pl.BlockSpec((tm, MLP_WIDTH), row),
                  _resident((MLP_GROUPS, CHUNK, CHUNK), lambda i: (0, 0, 0)),
                  _resident((CHUNK, MLP_WIDTH), lambda i: (0, 0)),
                  _resident((1, MLP_WIDTH), lambda i: (0, 0))],
        out_specs=pl.BlockSpec((tm, MLP_WIDTH), row),
        out_shape=jax.ShapeDtypeStruct((S, MLP_WIDTH), BF16),
        scratch_shapes=[pltpu.VMEM((MLP_GROUPS, CHUNK, CHUNK), BF16)],
        compiler_params=pltpu.CompilerParams(dimension_semantics=("arbitrary",), vmem_limit_bytes=VMEM_LIMIT),
        name="gmlp",
    )(u, v, ws, bs_exp, gain)


def _outproj_kernel(h_ref, a_ref, m_ref, ag_ref, wa_ref, wm_ref, pg_ref, out_ref):
    an = _rms(a_ref[...], ag_ref[...]).astype(BF16)
    mix = (jnp.dot(an, wa_ref[...], preferred_element_type=F32)
           + jnp.dot(m_ref[...], wm_ref[...], preferred_element_type=F32))
    out_ref[...] = h_ref[...] + _rms(mix, pg_ref[...])


def _outproj(h, attn, mlpn, ag, wa, wm, pg, tm=1024):
    S, D = h.shape
    row = lambda i: (i, 0)
    fix = lambda i: (0, 0)
    return pl.pallas_call(
        _outproj_kernel,
        grid=(S // tm,),
        in_specs=[pl.BlockSpec((tm, D), row), pl.BlockSpec((tm, ATTN_WIDTH), row), pl.BlockSpec((tm, MLP_WIDTH), row),
                  _resident((1, ATTN_WIDTH), fix), _resident((ATTN_WIDTH, D), fix),
                  _resident((MLP_WIDTH, D), fix), _resident((1, D), fix)],
        out_specs=pl.BlockSpec((tm, D), row),
        out_shape=jax.ShapeDtypeStruct((S, D), F32),
        compiler_params=pltpu.CompilerParams(dimension_semantics=("arbitrary",), vmem_limit_bytes=VMEM_LIMIT),
        name="outproj",
    )(h, attn, mlpn, ag, wa, wm, pg)


HALO = BF16_ROWS


def _ffn_kernel(h_ref, halo_ref, p_ref, g_ref, wup_ref, cw_ref, cb_ref, wd_ref, pg_ref, plg_ref, wpg_ref, wpp_ref,
                out_ref, acc_ref, *, fc):
    dff = wd_ref.shape[0]
    h = h_ref[...]
    hn = _rms(halo_ref[...], g_ref[...])
    hn = jnp.where(pl.program_id(0) == 0, 0.0, hn)
    xn = jnp.concatenate([hn.astype(BF16), _rms(h, g_ref[...]).astype(BF16)], axis=0)

    def conv(lo, n):
        hh = jnp.dot(xn, wup_ref[:, lo:lo + n], preferred_element_type=F32)
        cw = cw_ref[:, lo:lo + n]
        y = cb_ref[:, lo:lo + n] + pltpu.roll(hh, 2, 0) * cw[0:1] + pltpu.roll(hh, 1, 0) * cw[1:2] + hh * cw[2:3]
        return y[HALO:, :]

    for c0 in range(0, dff, fc):
        n = min(fc, dff - c0)
        act = (jax.nn.silu(conv(c0, n)) * conv(dff + c0, n)).astype(BF16)
        part = jnp.dot(act, wd_ref[c0:c0 + n, :], preferred_element_type=F32)
        if c0 == 0:
            acc_ref[...] = part
        else:
            acc_ref[...] += part

    h2 = h + _rms(acc_ref[...], pg_ref[...])
    gate = jax.nn.sigmoid(jnp.dot(_rms(h2, plg_ref[...]).astype(BF16), wpg_ref[...], preferred_element_type=F32))
    proj = jnp.dot(p_ref[...].astype(BF16), wpp_ref[...], preferred_element_type=F32)
    out_ref[...] = h2 + gate * proj


def _ffn(h, p, gain, w_up, conv_w, conv_b, w_down, pg, plg, wpg, wpp, tm=1024, fc=1024):
    S, D = h.shape
    dff = w_down.shape[0]
    P = p.shape[1]
    row = lambda i: (i, 0)
    fix = lambda i: (0, 0)
    halo = lambda i: (jnp.maximum(i * (tm // HALO) - 1, 0), 0)
    return pl.pallas_call(
        functools.partial(_ffn_kernel, fc=fc),
        grid=(S // tm,),
        in_specs=[pl.BlockSpec((tm, D), row), pl.BlockSpec((HALO, D), halo), pl.BlockSpec((tm, P), row),
                  _resident((1, D), fix), _resident((D, 2 * dff), fix), _resident((CONV_WIDTH, 2 * dff), fix),
                  _resident((1, 2 * dff), fix), _resident((dff, D), fix), _resident((1, D), fix),
                  _resident((1, D), fix), _resident((D, D), fix), _resident((P, D), fix)],
        out_specs=pl.BlockSpec((tm, D), row),
        out_shape=jax.ShapeDtypeStruct((S, D), F32),
        scratch_shapes=[pltpu.VMEM((tm, D), F32)],
        compiler_params=pltpu.CompilerParams(dimension_semantics=("arbitrary",), vmem_limit_bytes=VMEM_LIMIT),
        name="convffn_ple",
    )(h, h, p, gain, w_up, conv_w, conv_b, w_down, pg, plg, wpg, wpp)


def _rope_tables(pos):
    half = ROT_DIM // 2
    inv = ROPE_THETA ** (-np.arange(half, dtype=np.float64) / half)
    ang = np.asarray(pos, np.float64)[:, None] * inv[None, :]
    cos, sin = jnp.asarray(np.cos(ang), dtype=F32), jnp.asarray(np.sin(ang), dtype=F32)
    n = ang.shape[0]
    one = jnp.ones((n, HEAD_DIM - ROT_DIM), F32)
    zero = jnp.zeros((n, HEAD_DIM - ROT_DIM), F32)
    zh = jnp.zeros((n, half), F32)
    c = jnp.concatenate([cos, cos, one], axis=1)
    sm = jnp.concatenate([-sin, zh, zero], axis=1)
    sp = jnp.concatenate([zh, sin, zero], axis=1)
    dup = lambda t: jnp.concatenate([t, t], axis=1)
    return dup(c), dup(sm), dup(sp)


def _mask_tables(ncp):
    qo = np.arange(Q_TILE)[None, :]
    rel = (np.arange(2 * ncp) - ncp)[:, None]
    cmp_ok = rel * CMP_STRIDE + CMP_LEN - 1 <= qo
    r = np.arange(WINDOW + Q_TILE)[:, None]
    win_ok = []
    for v in range(WIN_EARLY):
        tq = v * Q_TILE + qo
        win_ok.append((r <= tq) & (r > tq - WINDOW))
    win_ok.append((r - WINDOW <= qo) & (r > qo))
    loc_ok = np.arange(Q_TILE)[:, None] <= qo
    tab = lambda ok: jnp.asarray(np.where(ok, 0.0, NEG), dtype=F32)
    return tab(cmp_ok), tab(np.stack(win_ok)), tab(loc_ok)


def _gate_columns():
    src = np.full((LANES,), -1)
    for g in range(N_GROUPS):
        for br in range(N_BRANCH):
            for r in range(GQA):
                src[g * GATE_SLOT + br * GQA + r] = (g * GQA + r) * N_BRANCH + br
    return src


def _layer(h, p, prm, tabs):
    S, D = h.shape
    rc, rm, rp, crc, crm, crp, cbt, wb, lb = tabs

    sizes = [ATTN_WIDTH] + [KV_WIDTH] * 6 + [N_HEADS * N_BRANCH, MLP_WIDTH, MLP_WIDTH]
    offs = np.concatenate([[0], np.cumsum(sizes)])
    w_in = prm["w_in"]
    seg = lambda k: w_in[:, offs[k]:offs[k + 1]]
    src = _gate_columns()
    wg = jnp.where(jnp.asarray(src >= 0)[None, :], seg(7)[:, np.maximum(src, 0)], 0.0)
    w_cat = jnp.concatenate([seg(0)] + [seg(k) for k in range(1, 7)] + [seg(8), seg(9), wg], axis=1).astype(BF16)

    q, ksa, kwa, vsT, vwT, kcx, vcx, u, v, gates = _inproj(h, prm["pre_mix_g"][None], w_cat, rc, rm, rp,
                                                           prm["gmlp_ln_g"][None], prm["gmlp_ln_b"][None])

    w1 = prm["cmp_w1"].astype(BF16)
    half = CMP_STRIDE * HEAD_DIM
    w1l = jnp.concatenate([w1[:, :half], w1[:, half:]], axis=2).reshape(2, CMP_STRIDE, HEAD_DIM, 2 * CMP_HIDDEN)
    zl = jnp.zeros_like(w1l)
    wboth = jnp.concatenate([jnp.concatenate([w1l, zl], axis=3), jnp.concatenate([zl, w1l], axis=3)], axis=2)
    wboth = wboth.reshape(2, CMP_STRIDE * LANES, 2 * N_GROUPS * CMP_HIDDEN)
    pe = jnp.broadcast_to(prm["cmp_pe"].reshape(2, 1, CMP_LEN * HEAD_DIM), (2, 8, CMP_LEN * HEAD_DIM)).astype(BF16)
    w2 = jnp.pad(prm["cmp_w2"], ((0, 0), (0, 0), (0, LANES - HEAD_DIM))).astype(BF16)
    b2 = jnp.pad(prm["cmp_b2"], ((0, 0), (0, LANES - HEAD_DIM)))[:, None]
    kc, vcT = _compress(kcx, vcx, wboth, w1, pe, prm["cmp_b1"][:, None], w2, b2, crc, crm, crp)

    attn = _attention(q, gates, ksa, vsT, kwa, vwT, kc, vcT, cbt, wb, lb)

    bs_exp = jnp.repeat(prm["gmlp_bs"].T, HEAD_DIM, axis=1)
    mlpn = _gmlp(u, v, prm["gmlp_ws"], bs_exp, prm["mlp_out_g"][None])

    w_o = prm["w_o"].astype(BF16)
    h = _outproj(h, attn, mlpn, prm["attn_out_g"][None], w_o[:ATTN_WIDTH], w_o[ATTN_WIDTH:], prm["post_mix_g"][None])
    h = _ffn(h, p, prm["pre_ffn_g"][None], prm["w_up"].astype(BF16), prm["conv_w"], prm["conv_b"][None],
             prm["w_down"].astype(BF16), prm["post_ffn_g"][None], prm["ple_norm_g"][None],
             prm["w_ple_gate"].astype(BF16), prm["w_ple_proj"].astype(BF16))
    return h


def kernel(x, p, pre_mix_g, w_in, cmp_pe, cmp_w1, cmp_b1, cmp_w2, cmp_b2, gmlp_ln_g, gmlp_ln_b, gmlp_ws, gmlp_bs,
           attn_out_g, mlp_out_g, w_o, post_mix_g, pre_ffn_g, w_up, conv_w, conv_b, w_down, post_ffn_g,
           ple_norm_g, w_ple_gate, w_ple_proj):
    B, S, D = x.shape
    depth = p.shape[0]
    assert S % (BIAS_WIN * SEL_LEN) == 0 and D % LANES == 0
    nch = S // CMP_STRIDE
    rc, rm, rp = _rope_tables(np.arange(S))
    crc, crm, crp = _rope_tables(np.arange(nch) * CMP_STRIDE + CMP_LEN - 1)
    tabs = (rc, rm, rp, crc, crm, crp) + _mask_tables(nch)
    stacked = dict(pre_mix_g=pre_mix_g, w_in=w_in, cmp_pe=cmp_pe, cmp_w1=cmp_w1, cmp_b1=cmp_b1, cmp_w2=cmp_w2,
                   cmp_b2=cmp_b2, gmlp_ln_g=gmlp_ln_g, gmlp_ln_b=gmlp_ln_b, gmlp_ws=gmlp_ws, gmlp_bs=gmlp_bs,
                   attn_out_g=attn_out_g, mlp_out_g=mlp_out_g, w_o=w_o, post_mix_g=post_mix_g, pre_ffn_g=pre_ffn_g,
                   w_up=w_up, conv_w=conv_w, conv_b=conv_b, w_down=w_down, post_ffn_g=post_ffn_g,
                   ple_norm_g=ple_norm_g, w_ple_gate=w_ple_gate, w_ple_proj=w_ple_proj)
    outs = []
    for bi in range(B):
        h = x[bi]
        for i in range(depth):
            h = _layer(h, p[i, bi], {k: v[i] for k, v in stacked.items()}, tabs)
        outs.append(h)
    return jnp.stack(outs)
```

```python
import functools
import math

import numpy as np
import jax
import jax.numpy as jnp
from jax import lax
from jax.experimental import pallas as pl
from jax.experimental.pallas import tpu as pltpu

F32 = jnp.float32
BF16 = jnp.bfloat16

HEAD_DIM = 64
N_HEADS = 8
N_GROUPS = 2
GQA = N_HEADS // N_GROUPS
N_BRANCH = 3
ATTN_WIDTH = N_HEADS * HEAD_DIM
MLP_GROUPS = 8
MLP_WIDTH = MLP_GROUPS * HEAD_DIM
KV_WIDTH = N_GROUPS * HEAD_DIM
ROT_DIM = HEAD_DIM // 4
ROPE_THETA = 500000.0
CMP_LEN = 32
CMP_STRIDE = 16
CMP_HIDDEN = 256
SEL_LEN = 64
N_SELECT = 16
WINDOW = 512
CHUNK = 128
CONV_WIDTH = 3
NORM_EPS = 1e-6
NEG = -1e30
SCALE = HEAD_DIM ** -0.5
QSCALE = SCALE * math.log2(math.e)

LANES = 128
BF16_ROWS = 16
Q_TILE = 128
ROWS = GQA * Q_TILE
N_COLS = N_GROUPS * ROWS
KV_TILE = 1024
BIAS_WIN = 128
V_ROWS = HEAD_DIM + BF16_ROWS
CMP_PER_Q = Q_TILE // CMP_STRIDE
SEL_PER_Q = Q_TILE // SEL_LEN
CMP_CLASSES = 8
N_FORCED = 3
GATE_SLOT = 16
WIN_EARLY = WINDOW // Q_TILE
IMP_TAPS = range(-1, SEL_LEN // CMP_STRIDE)
VMEM_LIMIT = 56 * 1024 * 1024


def _rms(x, gain):
    return x * lax.rsqrt(jnp.mean(x * x, axis=-1, keepdims=True) + NORM_EPS) * gain


def _rope_slab(x, c, sm, sp):
    return x * c + pltpu.roll(x, LANES - ROT_DIM // 2, 1) * sm + pltpu.roll(x, ROT_DIM // 2, 1) * sp


def _colmax(s):
    return jnp.max(s, axis=0, keepdims=True)


def _tile_heads(x):
    return jnp.concatenate([x] * GQA, axis=1)


def _resident(shape, index_map):
    return pl.BlockSpec(shape, index_map, pipeline_mode=pl.Buffered(1))


def _inproj_kernel(x_ref, g_ref, w_ref, rc_ref, rm_ref, rp_ref, lng_ref, lnb_ref,
                   q_ref, ksa_ref, kwa_ref, vsT_ref, vwT_ref, kcx_ref, vcx_ref, u_ref, v_ref, gate_ref, craw_ref):
    tm = x_ref.shape[0]
    a = _rms(x_ref[...], g_ref[...]).astype(BF16)
    c, sm, sp = rc_ref[...], rm_ref[...], rp_ref[...]
    zq = jnp.dot(a, w_ref[:, 0:ATTN_WIDTH], preferred_element_type=F32)
    for j in range(ATTN_WIDTH // LANES):
        sl = slice(LANES * j, LANES * (j + 1))
        q_ref[:, sl] = (_rope_slab(zq[:, sl], c, sm, sp) * QSCALE).astype(BF16)
    o = ATTN_WIDTH
    zkv = jnp.dot(a, w_ref[:, o:o + 6 * KV_WIDTH], preferred_element_type=F32)
    slab = lambda j: zkv[:, LANES * j:LANES * (j + 1)]

    craw_ref[0] = slab(0)
    craw_ref[1] = slab(1)
    nch = tm // CMP_STRIDE
    for l in range(CMP_STRIDE):
        kcx_ref[l] = craw_ref[0, pl.ds(l, nch, stride=CMP_STRIDE), :].astype(BF16)
        vcx_ref[l] = craw_ref[1, pl.ds(l, nch, stride=CMP_STRIDE), :].astype(BF16)

    pos = pl.program_id(0) * tm + lax.broadcasted_iota(jnp.int32, (tm, BIAS_WIN), 0)
    lane = lax.broadcasted_iota(jnp.int32, (tm, BIAS_WIN), 1)
    ksa_ref[:, 0:LANES] = _rope_slab(slab(2), c, sm, sp).astype(BF16)
    ksa_ref[:, LANES:] = jnp.where((pos // SEL_LEN) % BIAS_WIN == lane, 1.0, 0.0).astype(BF16)
    kwa_ref[...] = _rope_slab(slab(4), c, sm, sp).astype(BF16)
    for k in range(tm // LANES):
        rs = slice(LANES * k, LANES * (k + 1))
        for ref, sl in ((vsT_ref, slab(3)), (vwT_ref, slab(5))):
            t = sl[rs, :].T.astype(BF16)
            for g in range(N_GROUPS):
                ref[g, 0:HEAD_DIM, rs] = t[HEAD_DIM * g:HEAD_DIM * (g + 1)]
                ref[g, HEAD_DIM:, rs] = jnp.ones((BF16_ROWS, LANES), BF16)

    o += 6 * KV_WIDTH
    zu = jnp.dot(a, w_ref[:, o:o + MLP_WIDTH], preferred_element_type=F32)
    u_ref[...] = jax.nn.gelu(zu).astype(BF16)
    o += MLP_WIDTH
    zv = jax.nn.gelu(jnp.dot(a, w_ref[:, o:o + MLP_WIDTH], preferred_element_type=F32))
    mu = jnp.mean(zv, axis=-1, keepdims=True)
    d = zv - mu
    var = jnp.mean(d * d, axis=-1, keepdims=True)
    v_ref[...] = (d * lax.rsqrt(var + NORM_EPS) * lng_ref[...] + lnb_ref[...]).astype(BF16)
    o += MLP_WIDTH
    zg = jnp.dot(a, w_ref[:, o:o + LANES], preferred_element_type=F32)
    gate_ref[...] = jax.nn.sigmoid(zg)


def _inproj(h, gain, w, rc, rm, rp, lng, lnb, tm=1024):
    S, D = h.shape
    N = w.shape[1]
    nch = S // CMP_STRIDE
    row = lambda i: (i, 0)
    fix = lambda i: (0, 0)
    chunked = lambda i: (0, i, 0)
    bf = lambda *shape: jax.ShapeDtypeStruct(shape, BF16)
    return pl.pallas_call(
        _inproj_kernel,
        grid=(S // tm,),
        in_specs=[pl.BlockSpec((tm, D), row), _resident((1, D), fix), _resident((D, N), fix),
                  pl.BlockSpec((tm, LANES), row), pl.BlockSpec((tm, LANES), row), pl.BlockSpec((tm, LANES), row),
                  _resident((1, MLP_WIDTH), fix), _resident((1, MLP_WIDTH), fix)],
        out_specs=[pl.BlockSpec((tm, ATTN_WIDTH), row), pl.BlockSpec((tm, 2 * LANES), row),
                   pl.BlockSpec((tm, LANES), row), pl.BlockSpec((N_GROUPS, V_ROWS, tm), lambda i: (0, 0, i)),
                   pl.BlockSpec((N_GROUPS, V_ROWS, tm), lambda i: (0, 0, i)),
                   pl.BlockSpec((CMP_STRIDE, tm // CMP_STRIDE, LANES), chunked),
                   pl.BlockSpec((CMP_STRIDE, tm // CMP_STRIDE, LANES), chunked),
                   pl.BlockSpec((tm, MLP_WIDTH), row), pl.BlockSpec((tm, MLP_WIDTH), row),
                   pl.BlockSpec((tm, LANES), row)],
        out_shape=[bf(S, ATTN_WIDTH), bf(S, 2 * LANES), bf(S, LANES), bf(N_GROUPS, V_ROWS, S), bf(N_GROUPS, V_ROWS, S),
                   bf(CMP_STRIDE, nch, LANES), bf(CMP_STRIDE, nch, LANES), bf(S, MLP_WIDTH), bf(S, MLP_WIDTH),
                   jax.ShapeDtypeStruct((S, LANES), F32)],
        scratch_shapes=[pltpu.VMEM((2, tm, LANES), F32)],
        compiler_params=pltpu.CompilerParams(dimension_semantics=("arbitrary",), vmem_limit_bytes=VMEM_LIMIT),
        name="inproj",
    )(h, gain, w, rc, rm, rp, lng, lnb)


def _compress_kernel(xk_ref, xv_ref, wb_ref, w1_ref, pe_ref, b1_ref, w2_ref, b2_ref, rc_ref, rm_ref, rp_ref,
                     kc_ref, vcT_ref):
    nch = xk_ref.shape[1]

    def run(x_ref):
        x = jnp.concatenate([x_ref[l] for l in range(CMP_STRIDE)], axis=1)
        c = jnp.dot(x, wb_ref[0], preferred_element_type=F32)
        cvec = jnp.dot(pe_ref[0], w1_ref[0], preferred_element_type=F32)[0:1] + b1_ref[0]
        outs = []
        for g in range(N_GROUPS):
            cg = c[:, 2 * CMP_HIDDEN * g:2 * CMP_HIDDEN * (g + 1)]
            h = jax.nn.gelu(cg[:, :CMP_HIDDEN] + pltpu.roll(cg[:, CMP_HIDDEN:], nch - 1, 0) + cvec)
            outs.append(jnp.dot(h.astype(BF16), w2_ref[0], preferred_element_type=F32) + b2_ref[0])
        return outs

    @pl.when(pl.program_id(0) == 0)
    def _():
        o = run(xk_ref)
        shared = o[0]
        for g in range(1, N_GROUPS):
            shared = shared + pltpu.roll(o[g], HEAD_DIM * g, 1)
        kc_ref[...] = _rope_slab(shared, rc_ref[...], rm_ref[...], rp_ref[...]).astype(BF16)

    @pl.when(pl.program_id(0) == 1)
    def _():
        for g, o in enumerate(run(xv_ref)):
            for k in range(nch // LANES):
                rs = slice(LANES * k, LANES * (k + 1))
                vcT_ref[g, :, rs] = o[rs, :].T[0:HEAD_DIM].astype(BF16)


def _compress(kcx, vcx, wboth, w1, pe, b1, w2, b2, rc, rm, rp):
    _, nch, _ = kcx.shape
    cw = CMP_STRIDE * HEAD_DIM
    kv3 = lambda k: (k, 0, 0)
    fix3 = lambda k: (0, 0, 0)
    fix2 = lambda k: (0, 0)
    return pl.pallas_call(
        _compress_kernel,
        grid=(2,),
        in_specs=[_resident((CMP_STRIDE, nch, LANES), fix3), _resident((CMP_STRIDE, nch, LANES), fix3),
                  pl.BlockSpec((1, CMP_STRIDE * LANES, 2 * N_GROUPS * CMP_HIDDEN), kv3),
                  pl.BlockSpec((1, 2 * cw, CMP_HIDDEN), kv3), pl.BlockSpec((1, 8, 2 * cw), kv3),
                  pl.BlockSpec((1, 1, CMP_HIDDEN), kv3), pl.BlockSpec((1, CMP_HIDDEN, LANES), kv3),
                  pl.BlockSpec((1, 1, LANES), kv3),
                  _resident((nch, LANES), fix2), _resident((nch, LANES), fix2), _resident((nch, LANES), fix2)],
        out_specs=[pl.BlockSpec((nch, LANES), fix2), pl.BlockSpec((N_GROUPS, HEAD_DIM, nch), fix3)],
        out_shape=[jax.ShapeDtypeStruct((nch, LANES), BF16),
                   jax.ShapeDtypeStruct((N_GROUPS, HEAD_DIM, nch), BF16)],
        compiler_params=pltpu.CompilerParams(dimension_semantics=("arbitrary",), vmem_limit_bytes=VMEM_LIMIT),
        name="compress",
    )(kcx, vcx, wboth, w1, pe, b1, w2, b2, rc, rm, rp)


def _attn_kernel(q_ref, gate_ref, ksa_ref, vsT_ref, kwa_ref, vwT_ref, kc_ref, vcT_ref, cbt_ref, wb_ref, lb_ref,
                 out_ref, qaug_ref, sbuf_ref, acc_ref, ma_ref, ms_ref, ps_ref, oc_ref, ow_ref, sel_ref,
                 *, ns, ncp):
    b = pl.program_id(0)
    t0 = pl.multiple_of(b * Q_TILE, Q_TILE)
    gcols = lambda g: slice(ROWS * g, ROWS * (g + 1))
    gq = lambda g: slice(Q_TILE * g, Q_TILE * (g + 1))
    per_group = lambda f: jnp.concatenate([f(g) for g in range(N_GROUPS)], axis=1)

    qf = q_ref[...].astype(F32)
    parts = []
    for j in range(N_HEADS // 2):
        t = qf[:, LANES * j:LANES * (j + 1)].T
        parts += [t[0:HEAD_DIM], t[HEAD_DIM:]]
    qT = jnp.concatenate(parts, axis=1).astype(BF16)
    zeros = jnp.zeros((HEAD_DIM, ROWS), BF16)
    qgT = jnp.concatenate(
        [jnp.concatenate([qT[:, gcols(g)] if gg == g else zeros for gg in range(N_GROUPS)], axis=1)
         for g in range(N_GROUPS)], axis=0)

    step = ncp // CMP_CLASSES
    cls = (b * CMP_PER_Q + CMP_PER_Q - 2) // step
    for k in range(CMP_CLASSES):
        @pl.when(cls == k)
        def _(nk=(k + 1) * step):
            cb = _tile_heads(cbt_ref[pl.ds(pl.multiple_of(ncp - b * CMP_PER_Q, 8), nk), :])
            s = jnp.dot(kc_ref[0:nk, :], qgT, preferred_element_type=F32) + per_group(lambda g: cb)
            m = _colmax(s)
            e = jnp.exp2(s - m)
            inv = jnp.where(m > NEG * 0.5, 1.0 / jnp.maximum(jnp.sum(e, axis=0, keepdims=True), 1e-30), 0.0)
            p = e * inv
            for g in range(N_GROUPS):
                pg = p[:, gcols(g)]
                oc_ref[:, gcols(g)] = jnp.dot(vcT_ref[g, :, 0:nk], pg.astype(BF16), preferred_element_type=F32)
                ps = pg[:, 0:Q_TILE]
                for r in range(1, GQA):
                    ps = ps + pg[:, r * Q_TILE:(r + 1) * Q_TILE]
                ps_ref[g, 0:8, :] = jnp.zeros((8, Q_TILE), F32)
                ps_ref[g, 8:8 + nk, :] = ps
                if nk < ncp:
                    ps_ref[g, 8 + nk:, :] = jnp.zeros((ncp - nk, Q_TILE), F32)

    ratio = SEL_LEN // CMP_STRIDE
    imp = per_group(lambda g: sum(ps_ref[g, pl.ds(8 + d, ns, stride=ratio), :] for d in IMP_TAPS))

    wk = WINDOW + Q_TILE
    ws = pl.multiple_of(jnp.maximum(t0 - WINDOW, 0), Q_TILE)
    wb = _tile_heads(wb_ref[jnp.minimum(b, WIN_EARLY)])
    s = jnp.dot(kwa_ref[pl.ds(ws, wk), :], qgT, preferred_element_type=F32) + per_group(lambda g: wb)
    pw = jnp.exp2(s - _colmax(s)).astype(BF16)
    for g in range(N_GROUPS):
        ow = jnp.dot(vwT_ref[g, :, pl.ds(ws, wk)], pw[:, gcols(g)], preferred_element_type=F32)
        ow_ref[:, gcols(g)] = ow[0:HEAD_DIM] / jnp.maximum(ow[HEAD_DIM:HEAD_DIM + 1], 1e-30)

    lb = _tile_heads(lb_ref[...])
    s = (jnp.dot(ksa_ref[pl.ds(t0, Q_TILE), 0:LANES], qgT, preferred_element_type=F32)
         + per_group(lambda g: lb))
    m0 = _colmax(s)
    p0 = jnp.exp2(s - m0).astype(BF16)
    for g in range(N_GROUPS):
        acc_ref[g] = jnp.dot(vsT_ref[g, :, pl.ds(t0, Q_TILE)], p0[:, gcols(g)], preferred_element_type=F32)
    ma_ref[...] = m0
    ms_ref[...] = m0

    nq = N_GROUPS * Q_TILE
    blk = lax.broadcasted_iota(jnp.int32, (ns, nq), 0)
    cur = (t0 + (lax.broadcasted_iota(jnp.int32, (ns, nq), 1) & (Q_TILE - 1))) // SEL_LEN
    valid = blk <= cur
    forced = valid & ((blk == 0) | (blk == cur) | (blk == cur - 1))
    rank0 = jnp.where(valid & jnp.logical_not(forced), imp, NEG)
    blk_f = blk.astype(F32)
    taken = -(2.0 ** 126)

    def rank(break_ties):
        score = rank0
        for _ in range(N_SELECT - N_FORCED):
            mx = _colmax(score)
            if break_ties:
                idx = jnp.min(jnp.where(score == mx, blk_f, float(ns)), axis=0, keepdims=True)
                hit = blk_f == idx
            else:
                hit = score == mx
            score = jnp.where(hit, taken, score)
        return (score == taken) & (rank0 > NEG * 0.5)

    picked = rank(False)
    count = jnp.sum(jnp.where(picked, 1.0, 0.0), axis=0, keepdims=True)
    want = jnp.clip(cur[0:1] - (N_FORCED - 1), 0, N_SELECT - N_FORCED).astype(F32)
    sel_ref[...] = jnp.where(forced | picked, 1.0, 0.0)

    @pl.when(jnp.logical_not(jnp.all(count == want)))
    def _():
        sel_ref[...] = jnp.where(forced | rank(True), 1.0, 0.0)

    sel = sel_ref[...] > 0.5
    bias = jnp.where(sel & (blk < b * SEL_PER_Q), 0.0, NEG).astype(BF16)
    for w in range(ns // BIAS_WIN):
        qaug_ref[w, 0:2 * HEAD_DIM, :] = qgT
        qaug_ref[w, 2 * HEAD_DIM:, :] = per_group(
            lambda g: _tile_heads(bias[BIAS_WIN * w:BIAS_WIN * (w + 1), gq(g)]))

    n_tiles = jnp.maximum((t0 + KV_TILE - 1) // KV_TILE, 1)
    tiles_per_win = BIAS_WIN * SEL_LEN // KV_TILE

    def scores(t, slot):
        start = pl.multiple_of(t * KV_TILE, KV_TILE)
        s = jnp.dot(ksa_ref[pl.ds(start, KV_TILE), :], qaug_ref[t // tiles_per_win],
                    preferred_element_type=F32)
        sbuf_ref[slot] = s
        ms_ref[...] = jnp.maximum(ms_ref[...], _colmax(s))

    def consume(t, slot, m_s):
        start = pl.multiple_of(t * KV_TILE, KV_TILE)
        alpha = jnp.exp2(ma_ref[...] - m_s)
        pp = jnp.exp2(sbuf_ref[slot] - m_s).astype(BF16)
        for g in range(N_GROUPS):
            acc_ref[g] = alpha[:, gcols(g)] * acc_ref[g] + jnp.dot(
                vsT_ref[g, :, pl.ds(start, KV_TILE)], pp[:, gcols(g)], preferred_element_type=F32)
        ma_ref[...] = m_s

    def step_(t, slot):
        m_s = ms_ref[...]
        scores(t + 1, 1 - slot)
        consume(t, slot, m_s)

    scores(0, 0)

    def body(i, carry):
        step_(2 * i, 0)
        step_(2 * i + 1, 1)
        return carry

    n_steps = n_tiles - 1
    lax.fori_loop(0, n_steps // 2, body, 0)

    @pl.when(n_steps % 2 == 1)
    def _():
        step_(n_steps - 1, 0)

    consume(n_steps, n_steps % 2, ms_ref[...])

    owT = ow_ref[...]
    osT = per_group(lambda g: acc_ref[g, 0:HEAD_DIM, :] / jnp.maximum(acc_ref[g, HEAD_DIM:HEAD_DIM + 1, :], 1e-30))

    gall = gate_ref[...].T
    row = lambda i: gall[i:i + 1]
    gt = lambda br: jnp.concatenate([row(g * GATE_SLOT + br * GQA + r)
                                     for g in range(N_GROUPS) for r in range(GQA)], axis=1)
    oT = gt(0) * oc_ref[...] + gt(1) * osT + gt(2) * owT
    for j in range(N_HEADS // 2):
        pair = jnp.concatenate([oT[:, Q_TILE * 2 * j:Q_TILE * (2 * j + 1)],
                                oT[:, Q_TILE * (2 * j + 1):Q_TILE * (2 * j + 2)]], axis=0)
        out_ref[:, LANES * j:LANES * (j + 1)] = pair.T


def _attention(q, gates, ksa, vsT, kwa, vwT, kc, vcT, cbt, wb, lb):
    S = q.shape[0]
    nqb = S // Q_TILE
    ncp = kc.shape[0]
    ns = S // SEL_LEN
    assert KV_WIDTH == LANES
    row = lambda b: (b, 0)
    fix = lambda b: (0, 0)
    fix3 = lambda b: (0, 0, 0)
    kernel = functools.partial(_attn_kernel, ns=ns, ncp=ncp)
    return pl.pallas_call(
        kernel,
        grid=(nqb,),
        in_specs=[pl.BlockSpec((Q_TILE, ATTN_WIDTH), row), pl.BlockSpec((Q_TILE, LANES), row),
                  _resident((S, 2 * LANES), fix), _resident((N_GROUPS, V_ROWS, S), fix3),
                  _resident((S, LANES), fix), _resident((N_GROUPS, V_ROWS, S), fix3),
                  _resident((ncp, LANES), fix), _resident((N_GROUPS, HEAD_DIM, ncp), fix3),
                  _resident((2 * ncp, Q_TILE), fix),
                  _resident((WIN_EARLY + 1, WINDOW + Q_TILE, Q_TILE), fix3),
                  _resident((Q_TILE, Q_TILE), fix)],
        out_specs=pl.BlockSpec((Q_TILE, ATTN_WIDTH), row),
        out_shape=jax.ShapeDtypeStruct((S, ATTN_WIDTH), F32),
        scratch_shapes=[pltpu.VMEM((ns // BIAS_WIN, 2 * LANES, N_COLS), BF16),
                        pltpu.VMEM((2, KV_TILE, N_COLS), F32),
                        pltpu.VMEM((N_GROUPS, V_ROWS, ROWS), F32),
                        pltpu.VMEM((1, N_COLS), F32), pltpu.VMEM((1, N_COLS), F32),
                        pltpu.VMEM((N_GROUPS, 8 + ncp, Q_TILE), F32),
                        pltpu.VMEM((HEAD_DIM, N_COLS), F32),
                        pltpu.VMEM((HEAD_DIM, N_COLS), F32),
                        pltpu.VMEM((ns, N_GROUPS * Q_TILE), F32)],
        compiler_params=pltpu.CompilerParams(dimension_semantics=("arbitrary",), vmem_limit_bytes=VMEM_LIMIT),
        name="nsa_attention",
    )(q, gates, ksa, vsT, kwa, vwT, kc, vcT, cbt, wb, lb)


def _gmlp_kernel(u_ref, v_ref, ws_ref, bs_ref, g_ref, out_ref, wsm_ref, *, chunks):
    @pl.when(pl.program_id(0) == 0)
    def _():
        r = lax.broadcasted_iota(jnp.int32, (CHUNK, CHUNK), 0)
        c = lax.broadcasted_iota(jnp.int32, (CHUNK, CHUNK), 1)
        for g in range(MLP_GROUPS):
            wsm_ref[g] = jnp.where(c <= r, ws_ref[g], 0.0).astype(BF16)

    lane = lax.broadcasted_iota(jnp.int32, (CHUNK, LANES), 1)
    left = lane < HEAD_DIM
    for ci in range(chunks):
        rows = slice(ci * CHUNK, (ci + 1) * CHUNK)
        parts = []
        for pr in range(MLP_GROUPS // 2):
            vp = v_ref[rows, pr * LANES:(pr + 1) * LANES]
            va = jnp.where(left, vp, jnp.zeros_like(vp))
            vb = jnp.where(left, jnp.zeros_like(vp), vp)
            parts.append(jnp.dot(wsm_ref[2 * pr], va, preferred_element_type=F32)
                         + jnp.dot(wsm_ref[2 * pr + 1], vb, preferred_element_type=F32))
        mixed = jnp.concatenate(parts, axis=1) + bs_ref[...]
        y = u_ref[rows, :].astype(F32) * mixed
        out_ref[rows, :] = _rms(y, g_ref[...]).astype(BF16)


def _gmlp(u, v, ws, bs_exp, gain, chunks=8):
    S = u.shape[0]
    tm = CHUNK * chunks
    row = lambda i: (i, 0)
    return pl.pallas_call(
        functools.partial(_gmlp_kernel, chunks=chunks),
        grid=(S // tm,),
        in_specs=[pl.BlockSpec((tm, MLP_WIDTH), row), pl.BlockSpec((tm, MLP_WIDTH), row),
                  _resident((MLP_GROUPS, CHUNK, CHUNK), lambda i: (0, 0, 0)),
                  _resident((CHUNK, MLP_WIDTH), lambda i: (0, 0)),
                  _resident((1, MLP_WIDTH), lambda i: (0, 0))],
        out_specs=pl.BlockSpec((tm, MLP_WIDTH), row),
        out_shape=jax.ShapeDtypeStruct((S, MLP_WIDTH), BF16),
        scratch_shapes=[pltpu.VMEM((MLP_GROUPS, CHUNK, CHUNK), BF16)],
        compiler_params=pltpu.CompilerParams(dimension_semantics=("arbitrary",), vmem_limit_bytes=VMEM_LIMIT),
        name="gmlp",
    )(u, v, ws, bs_exp, gain)


def _outproj_kernel(h_ref, a_ref, m_ref, ag_ref, wa_ref, wm_ref, pg_ref, out_ref):
    an = _rms(a_ref[...], ag_ref[...]).astype(BF16)
    mix = (jnp.dot(an, wa_ref[...], preferred_element_type=F32)
           + jnp.dot(m_ref[...], wm_ref[...], preferred_element_type=F32))
    out_ref[...] = h_ref[...] + _rms(mix, pg_ref[...])


def _outproj(h, attn, mlpn, ag, wa, wm, pg, tm=1024):
    S, D = h.shape
    row = lambda i: (i, 0)
    fix = lambda i: (0, 0)
    return pl.pallas_call(
        _outproj_kernel,
        grid=(S // tm,),
        in_specs=[pl.BlockSpec((tm, D), row), pl.BlockSpec((tm, ATTN_WIDTH), row), pl.BlockSpec((tm, MLP_WIDTH), row),
                  _resident((1, ATTN_WIDTH), fix), _resident((ATTN_WIDTH, D), fix),
                  _resident((MLP_WIDTH, D), fix), _resident((1, D), fix)],
        out_specs=pl.BlockSpec((tm, D), row),
        out_shape=jax.ShapeDtypeStruct((S, D), F32),
        compiler_params=pltpu.CompilerParams(dimension_semantics=("arbitrary",), vmem_limit_bytes=VMEM_LIMIT),
        name="outproj",
    )(h, attn, mlpn, ag, wa, wm, pg)


HALO = BF16_ROWS


def _ffn_kernel(h_ref, halo_ref, p_ref, g_ref, wup_ref, cw_ref, cb_ref, wd_ref, pg_ref, plg_ref, wpg_ref, wpp_ref,
                out_ref, acc_ref, *, fc):
    dff = wd_ref.shape[0]
    h = h_ref[...]
    hn = _rms(halo_ref[...], g_ref[...])
    hn = jnp.where(pl.program_id(0) == 0, 0.0, hn)
    xn = jnp.concatenate([hn.astype(BF16), _rms(h, g_ref[...]).astype(BF16)], axis=0)

    def conv(lo, n):
        hh = jnp.dot(xn, wup_ref[:, lo:lo + n], preferred_element_type=F32)
        cw = cw_ref[:, lo:lo + n]
        y = cb_ref[:, lo:lo + n] + pltpu.roll(hh, 2, 0) * cw[0:1] + pltpu.roll(hh, 1, 0) * cw[1:2] + hh * cw[2:3]
        return y[HALO:, :]

    for c0 in range(0, dff, fc):
        n = min(fc, dff - c0)
        act = (jax.nn.silu(conv(c0, n)) * conv(dff + c0, n)).astype(BF16)
        part = jnp.dot(act, wd_ref[c0:c0 + n, :], preferred_element_type=F32)
        if c0 == 0:
            acc_ref[...] = part
        else:
            acc_ref[...] += part

    h2 = h + _rms(acc_ref[...], pg_ref[...])
    gate = jax.nn.sigmoid(jnp.dot(_rms(h2, plg_ref[...]).astype(BF16), wpg_ref[...], preferred_element_type=F32))
    proj = jnp.dot(p_ref[...].astype(BF16), wpp_ref[...], preferred_element_type=F32)
    out_ref[...] = h2 + gate * proj


def _ffn(h, p, gain, w_up, conv_w, conv_b, w_down, pg, plg, wpg, wpp, tm=1024, fc=1024):
    S, D = h.shape
    dff = w_down.shape[0]
    P = p.shape[1]
    row = lambda i: (i, 0)
    fix = lambda i: (0, 0)
    halo = lambda i: (jnp.maximum(i * (tm // HALO) - 1, 0), 0)
    return pl.pallas_call(
        functools.partial(_ffn_kernel, fc=fc),
        grid=(S // tm,),
        in_specs=[pl.BlockSpec((tm, D), row), pl.BlockSpec((HALO, D), halo), pl.BlockSpec((tm, P), row),
                  _resident((1, D), fix), _resident((D, 2 * dff), fix), _resident((CONV_WIDTH, 2 * dff), fix),
                  _resident((1, 2 * dff), fix), _resident((dff, D), fix), _resident((1, D), fix),
                  _resident((1, D), fix), _resident((D, D), fix), _resident((P, D), fix)],
        out_specs=pl.BlockSpec((tm, D), row),
        out_shape=jax.ShapeDtypeStruct((S, D), F32),
        scratch_shapes=[pltpu.VMEM((tm, D), F32)],
        compiler_params=pltpu.CompilerParams(dimension_semantics=("arbitrary",), vmem_limit_bytes=VMEM_LIMIT),
        name="convffn_ple",
    )(h, h, p, gain, w_up, conv_w, conv_b, w_down, pg, plg, wpg, wpp)


def _rope_tables(pos):
    half = ROT_DIM // 2
    inv = ROPE_THETA ** (-np.arange(half, dtype=np.float64) / half)
    ang = np.asarray(pos, np.float64)[:, None] * inv[None, :]
    cos, sin = jnp.asarray(np.cos(ang), dtype=F32), jnp.asarray(np.sin(ang), dtype=F32)
    n = ang.shape[0]
    one = jnp.ones((n, HEAD_DIM - ROT_DIM), F32)
    zero = jnp.zeros((n, HEAD_DIM - ROT_DIM), F32)
    zh = jnp.zeros((n, half), F32)
    c = jnp.concatenate([cos, cos, one], axis=1)
    sm = jnp.concatenate([-sin, zh, zero], axis=1)
    sp = jnp.concatenate([zh, sin, zero], axis=1)
    dup = lambda t: jnp.concatenate([t, t], axis=1)
    return dup(c), dup(sm), dup(sp)


def _mask_tables(ncp):
    qo = np.arange(Q_TILE)[None, :]
    rel = (np.arange(2 * ncp) - ncp)[:, None]
    cmp_ok = rel * CMP_STRIDE + CMP_LEN - 1 <= qo
    r = np.arange(WINDOW + Q_TILE)[:, None]
    win_ok = []
    for v in range(WIN_EARLY):
        tq = v * Q_TILE + qo
        win_ok.append((r <= tq) & (r > tq - WINDOW))
    win_ok.append((r - WINDOW <= qo) & (r > qo))
    loc_ok = np.arange(Q_TILE)[:, None] <= qo
    tab = lambda ok: jnp.asarray(np.where(ok, 0.0, NEG), dtype=F32)
    return tab(cmp_ok), tab(np.stack(win_ok)), tab(loc_ok)


def _gate_columns():
    src = np.full((LANES,), -1)
    for g in range(N_GROUPS):
        for br in range(N_BRANCH):
            for r in range(GQA):
                src[g * GATE_SLOT + br * GQA + r] = (g * GQA + r) * N_BRANCH + br
    return src


def _layer(h, p, prm, tabs):
    S, D = h.shape
    rc, rm, rp, crc, crm, crp, cbt, wb, lb = tabs

    sizes = [ATTN_WIDTH] + [KV_WIDTH] * 6 + [N_HEADS * N_BRANCH, MLP_WIDTH, MLP_WIDTH]
    offs = np.concatenate([[0], np.cumsum(sizes)])
    w_in = prm["w_in"]
    seg = lambda k: w_in[:, offs[k]:offs[k + 1]]
    src = _gate_columns()
    wg = jnp.where(jnp.asarray(src >= 0)[None, :], seg(7)[:, np.maximum(src, 0)], 0.0)
    w_cat = jnp.concatenate([seg(0)] + [seg(k) for k in range(1, 7)] + [seg(8), seg(9), wg], axis=1).astype(BF16)

    q, ksa, kwa, vsT, vwT, kcx, vcx, u, v, gates = _inproj(h, prm["pre_mix_g"][None], w_cat, rc, rm, rp,
                                                           prm["gmlp_ln_g"][None], prm["gmlp_ln_b"][None])

    w1 = prm["cmp_w1"].astype(BF16)
    half = CMP_STRIDE * HEAD_DIM
    w1l = jnp.concatenate([w1[:, :half], w1[:, half:]], axis=2).reshape(2, CMP_STRIDE, HEAD_DIM, 2 * CMP_HIDDEN)
    zl = jnp.zeros_like(w1l)
    wboth = jnp.concatenate([jnp.concatenate([w1l, zl], axis=3), jnp.concatenate([zl, w1l], axis=3)], axis=2)
    wboth = wboth.reshape(2, CMP_STRIDE * LANES, 2 * N_GROUPS * CMP_HIDDEN)
    pe = jnp.broadcast_to(prm["cmp_pe"].reshape(2, 1, CMP_LEN * HEAD_DIM), (2, 8, CMP_LEN * HEAD_DIM)).astype(BF16)
    w2 = jnp.pad(prm["cmp_w2"], ((0, 0), (0, 0), (0, LANES - HEAD_DIM))).astype(BF16)
    b2 = jnp.pad(prm["cmp_b2"], ((0, 0), (0, LANES - HEAD_DIM)))[:, None]
    kc, vcT = _compress(kcx, vcx, wboth, w1, pe, prm["cmp_b1"][:, None], w2, b2, crc, crm, crp)

    attn = _attention(q, gates, ksa, vsT, kwa, vwT, kc, vcT, cbt, wb, lb)

    bs_exp = jnp.repeat(prm["gmlp_bs"].T, HEAD_DIM, axis=1)
    mlpn = _gmlp(u, v, prm["gmlp_ws"], bs_exp, prm["mlp_out_g"][None])

    w_o = prm["w_o"].astype(BF16)
    h = _outproj(h, attn, mlpn, prm["attn_out_g"][None], w_o[:ATTN_WIDTH], w_o[ATTN_WIDTH:], prm["post_mix_g"][None])
    h = _ffn(h, p, prm["pre_ffn_g"][None], prm["w_up"].astype(BF16), prm["conv_w"], prm["conv_b"][None],
             prm["w_down"].astype(BF16), prm["post_ffn_g"][None], prm["ple_norm_g"][None],
             prm["w_ple_gate"].astype(BF16), prm["w_ple_proj"].astype(BF16))
    return h


def kernel(x, p, pre_mix_g, w_in, cmp_pe, cmp_w1, cmp_b1, cmp_w2, cmp_b2, gmlp_ln_g, gmlp_ln_b, gmlp_ws, gmlp_bs,
           attn_out_g, mlp_out_g, w_o, post_mix_g, pre_ffn_g, w_up, conv_w, conv_b, w_down, post_ffn_g,
           ple_norm_g, w_ple_gate, w_ple_proj):
    B, S, D = x.shape
    depth = p.shape[0]
    assert S % (BIAS_WIN * SEL_LEN) == 0 and D % LANES == 0
    nch = S // CMP_STRIDE
    rc, rm, rp = _rope_tables(np.arange(S))
    crc, crm, crp = _rope_tables(np.arange(nch) * CMP_STRIDE + CMP_LEN - 1)
    tabs = (rc, rm, rp, crc, crm, crp) + _mask_tables(nch)
    stacked = dict(pre_mix_g=pre_mix_g, w_in=w_in, cmp_pe=cmp_pe, cmp_w1=cmp_w1, cmp_b1=cmp_b1, cmp_w2=cmp_w2,
                   cmp_b2=cmp_b2, gmlp_ln_g=gmlp_ln_g, gmlp_ln_b=gmlp_ln_b, gmlp_ws=gmlp_ws, gmlp_bs=gmlp_bs,
                   attn_out_g=attn_out_g, mlp_out_g=mlp_out_g, w_o=w_o, post_mix_g=post_mix_g, pre_ffn_g=pre_ffn_g,
                   w_up=w_up, conv_w=conv_w, conv_b=conv_b, w_down=w_down, post_ffn_g=post_ffn_g,
                   ple_norm_g=ple_norm_g, w_ple_gate=w_ple_gate, w_ple_proj=w_ple_proj)
    outs = []
    for bi in range(B):
        h = x[bi]
        for i in range(depth):
            h = _layer(h, p[i, bi], {k: v[i] for k, v in stacked.items()}, tabs)
        outs.append(h)
    return jnp.stack(outs)
```

```python
import functools
import math

import numpy as np
import jax
import jax.numpy as jnp
from jax import lax
from jax.experimental import pallas as pl
from jax.experimental.pallas import tpu as pltpu

F32 = jnp.float32
BF16 = jnp.bfloat16

HEAD_DIM = 64
N_HEADS = 8
N_GROUPS = 2
GQA = N_HEADS // N_GROUPS
N_BRANCH = 3
ATTN_WIDTH = N_HEADS * HEAD_DIM
MLP_GROUPS = 8
MLP_WIDTH = MLP_GROUPS * HEAD_DIM
KV_WIDTH = N_GROUPS * HEAD_DIM
ROT_DIM = HEAD_DIM // 4
ROPE_THETA = 500000.0
CMP_LEN = 32
CMP_STRIDE = 16
CMP_HIDDEN = 256
SEL_LEN = 64
N_SELECT = 16
WINDOW = 512
CHUNK = 128
CONV_WIDTH = 3
NORM_EPS = 1e-6
NEG = -1e30
SCALE = HEAD_DIM ** -0.5
QSCALE = SCALE * math.log2(math.e)

LANES = 128
BF16_ROWS = 16
Q_TILE = 128
ROWS = GQA * Q_TILE
N_COLS = N_GROUPS * ROWS
KV_TILE = 1024
BIAS_WIN = 128
V_ROWS = HEAD_DIM + BF16_ROWS
CMP_PER_Q = Q_TILE // CMP_STRIDE
SEL_PER_Q = Q_TILE // SEL_LEN
CMP_CLASSES = 8
N_FORCED = 3
GATE_SLOT = 16
WIN_EARLY = WINDOW // Q_TILE
IMP_TAPS = range(-1, SEL_LEN // CMP_STRIDE)
VMEM_LIMIT = 56 * 1024 * 1024


def _rms(x, gain):
    return x * lax.rsqrt(jnp.mean(x * x, axis=-1, keepdims=True) + NORM_EPS) * gain


def _rope_slab(x, c, sm, sp):
    return x * c + pltpu.roll(x, LANES - ROT_DIM // 2, 1) * sm + pltpu.roll(x, ROT_DIM // 2, 1) * sp


def _colmax(s):
    return jnp.max(s, axis=0, keepdims=True)


def _tile_heads(x):
    return jnp.concatenate([x] * GQA, axis=1)


def _resident(shape, index_map):
    return pl.BlockSpec(shape, index_map, pipeline_mode=pl.Buffered(1))


def _inproj_kernel(x_ref, g_ref, w_ref, rc_ref, rm_ref, rp_ref, lng_ref, lnb_ref,
                   q_ref, ksa_ref, kwa_ref, vsT_ref, vwT_ref, kcx_ref, vcx_ref, u_ref, v_ref, gate_ref, craw_ref):
    tm = x_ref.shape[0]
    a = _rms(x_ref[...], g_ref[...]).astype(BF16)
    c, sm, sp = rc_ref[...], rm_ref[...], rp_ref[...]
    zq = jnp.dot(a, w_ref[:, 0:ATTN_WIDTH], preferred_element_type=F32)
    for j in range(ATTN_WIDTH // LANES):
        sl = slice(LANES * j, LANES * (j + 1))
        q_ref[:, sl] = (_rope_slab(zq[:, sl], c, sm, sp) * QSCALE).astype(BF16)
    o = ATTN_WIDTH
    zkv = jnp.dot(a, w_ref[:, o:o + 6 * KV_WIDTH], preferred_element_type=F32)
    slab = lambda j: zkv[:, LANES * j:LANES * (j + 1)]

    craw_ref[0] = slab(0)
    craw_ref[1] = slab(1)
    nch = tm // CMP_STRIDE
    for l in range(CMP_STRIDE):
        kcx_ref[l] = craw_ref[0, pl.ds(l, nch, stride=CMP_STRIDE), :].astype(BF16)
        vcx_ref[l] = craw_ref[1, pl.ds(l, nch, stride=CMP_STRIDE), :].astype(BF16)

    pos = pl.program_id(0) * tm + lax.broadcasted_iota(jnp.int32, (tm, BIAS_WIN), 0)
    lane = lax.broadcasted_iota(jnp.int32, (tm, BIAS_WIN), 1)
    ksa_ref[:, 0:LANES] = _rope_slab(slab(2), c, sm, sp).astype(BF16)
    ksa_ref[:, LANES:] = jnp.where((pos // SEL_LEN) % BIAS_WIN == lane, 1.0, 0.0).astype(BF16)
    kwa_ref[...] = _rope_slab(slab(4), c, sm, sp).astype(BF16)
    for k in range(tm // LANES):
        rs = slice(LANES * k, LANES * (k + 1))
        for ref, sl in ((vsT_ref, slab(3)), (vwT_ref, slab(5))):
            t = sl[rs, :].T.astype(BF16)
            for g in range(N_GROUPS):
                ref[g, 0:HEAD_DIM, rs] = t[HEAD_DIM * g:HEAD_DIM * (g + 1)]
                ref[g, HEAD_DIM:, rs] = jnp.ones((BF16_ROWS, LANES), BF16)

    o += 6 * KV_WIDTH
    zu = jnp.dot(a, w_ref[:, o:o + MLP_WIDTH], preferred_element_type=F32)
    u_ref[...] = jax.nn.gelu(zu).astype(BF16)
    o += MLP_WIDTH
    zv = jax.nn.gelu(jnp.dot(a, w_ref[:, o:o + MLP_WIDTH], preferred_element_type=F32))
    mu = jnp.mean(zv, axis=-1, keepdims=True)
    d = zv - mu
    var = jnp.mean(d * d, axis=-1, keepdims=True)
    v_ref[...] = (d * lax.rsqrt(var + NORM_EPS) * lng_ref[...] + lnb_ref[...]).astype(BF16)
    o += MLP_WIDTH
    zg = jnp.dot(a, w_ref[:, o:o + LANES], preferred_element_type=F32)
    gate_ref[...] = jax.nn.sigmoid(zg)


def _inproj(h, gain, w, rc, rm, rp, lng, lnb, tm=1024):
    S, D = h.shape
    N = w.shape[1]
    nch = S // CMP_STRIDE
    row = lambda i: (i, 0)
    fix = lambda i: (0, 0)
    chunked = lambda i: (0, i, 0)
    bf = lambda *shape: jax.ShapeDtypeStruct(shape, BF16)
    return pl.pallas_call(
        _inproj_kernel,
        grid=(S // tm,),
        in_specs=[pl.BlockSpec((tm, D), row), _resident((1, D), fix), _resident((D, N), fix),
                  pl.BlockSpec((tm, LANES), row), pl.BlockSpec((tm, LANES), row), pl.BlockSpec((tm, LANES), row),
                  _resident((1, MLP_WIDTH), fix), _resident((1, MLP_WIDTH), fix)],
        out_specs=[pl.BlockSpec((tm, ATTN_WIDTH), row), pl.BlockSpec((tm, 2 * LANES), row),
                   pl.BlockSpec((tm, LANES), row), pl.BlockSpec((N_GROUPS, V_ROWS, tm), lambda i: (0, 0, i)),
                   pl.BlockSpec((N_GROUPS, V_ROWS, tm), lambda i: (0, 0, i)),
                   pl.BlockSpec((CMP_STRIDE, tm // CMP_STRIDE, LANES), chunked),
                   pl.BlockSpec((CMP_STRIDE, tm // CMP_STRIDE, LANES), chunked),
                   pl.BlockSpec((tm, MLP_WIDTH), row), pl.BlockSpec((tm, MLP_WIDTH), row),
                   pl.BlockSpec((tm, LANES), row)],
        out_shape=[bf(S, ATTN_WIDTH), bf(S, 2 * LANES), bf(S, LANES), bf(N_GROUPS, V_ROWS, S), bf(N_GROUPS, V_ROWS, S),
                   bf(CMP_STRIDE, nch, LANES), bf(CMP_STRIDE, nch, LANES), bf(S, MLP_WIDTH), bf(S, MLP_WIDTH),
                   jax.ShapeDtypeStruct((S, LANES), F32)],
        scratch_shapes=[pltpu.VMEM((2, tm, LANES), F32)],
        compiler_params=pltpu.CompilerParams(dimension_semantics=("arbitrary",), vmem_limit_bytes=VMEM_LIMIT),
        name="inproj",
    )(h, gain, w, rc, rm, rp, lng, lnb)


def _compress_kernel(xk_ref, xv_ref, wb_ref, w1_ref, pe_ref, b1_ref, w2_ref, b2_ref, rc_ref, rm_ref, rp_ref,
                     kc_ref, vcT_ref):
    nch = xk_ref.shape[1]

    def run(x_ref):
        x = jnp.concatenate([x_ref[l] for l in range(CMP_STRIDE)], axis=1)
        c = jnp.dot(x, wb_ref[0], preferred_element_type=F32)
        cvec = jnp.dot(pe_ref[0], w1_ref[0], preferred_element_type=F32)[0:1] + b1_ref[0]
        outs = []
        for g in range(N_GROUPS):
            cg = c[:, 2 * CMP_HIDDEN * g:2 * CMP_HIDDEN * (g + 1)]
            h = jax.nn.gelu(cg[:, :CMP_HIDDEN] + pltpu.roll(cg[:, CMP_HIDDEN:], nch - 1, 0) + cvec)
            outs.append(jnp.dot(h.astype(BF16), w2_ref[0], preferred_element_type=F32) + b2_ref[0])
        return outs

    @pl.when(pl.program_id(0) == 0)
    def _():
        o = run(xk_ref)
        shared = o[0]
        for g in range(1, N_GROUPS):
            shared = shared + pltpu.roll(o[g], HEAD_DIM * g, 1)
        kc_ref[...] = _rope_slab(shared, rc_ref[...], rm_ref[...], rp_ref[...]).astype(BF16)

    @pl.when(pl.program_id(0) == 1)
    def _():
        for g, o in enumerate(run(xv_ref)):
            for k in range(nch // LANES):
                rs = slice(LANES * k, LANES * (k + 1))
                vcT_ref[g, :, rs] = o[rs, :].T[0:HEAD_DIM].astype(BF16)


def _compress(kcx, vcx, wboth, w1, pe, b1, w2, b2, rc, rm, rp):
    _, nch, _ = kcx.shape
    cw = CMP_STRIDE * HEAD_DIM
    kv3 = lambda k: (k, 0, 0)
    fix3 = lambda k: (0, 0, 0)
    fix2 = lambda k: (0, 0)
    return pl.pallas_call(
        _compress_kernel,
        grid=(2,),
        in_specs=[_resident((CMP_STRIDE, nch, LANES), fix3), _resident((CMP_STRIDE, nch, LANES), fix3),
                  pl.BlockSpec((1, CMP_STRIDE * LANES, 2 * N_GROUPS * CMP_HIDDEN), kv3),
                  pl.BlockSpec((1, 2 * cw, CMP_HIDDEN), kv3), pl.BlockSpec((1, 8, 2 * cw), kv3),
                  pl.BlockSpec((1, 1, CMP_HIDDEN), kv3), pl.BlockSpec((1, CMP_HIDDEN, LANES), kv3),
                  pl.BlockSpec((1, 1, LANES), kv3),
                  _resident((nch, LANES), fix2), _resident((nch, LANES), fix2), _resident((nch, LANES), fix2)],
        out_specs=[pl.BlockSpec((nch, LANES), fix2), pl.BlockSpec((N_GROUPS, HEAD_DIM, nch), fix3)],
        out_shape=[jax.ShapeDtypeStruct((nch, LANES), BF16),
                   jax.ShapeDtypeStruct((N_GROUPS, HEAD_DIM, nch), BF16)],
        compiler_params=pltpu.CompilerParams(dimension_semantics=("arbitrary",), vmem_limit_bytes=VMEM_LIMIT),
        name="compress",
    )(kcx, vcx, wboth, w1, pe, b1, w2, b2, rc, rm, rp)


def _attn_kernel(q_ref, gate_ref, ksa_ref, vsT_ref, kwa_ref, vwT_ref, kc_ref, vcT_ref, cbt_ref, wb_ref, lb_ref,
                 out_ref, qaug_ref, sbuf_ref, acc_ref, ma_ref, ms_ref, ps_ref, oc_ref, ow_ref, sel_ref,
                 *, ns, ncp):
    b = pl.program_id(0)
    t0 = pl.multiple_of(b * Q_TILE, Q_TILE)
    gcols = lambda g: slice(ROWS * g, ROWS * (g + 1))
    gq = lambda g: slice(Q_TILE * g, Q_TILE * (g + 1))
    per_group = lambda f: jnp.concatenate([f(g) for g in range(N_GROUPS)], axis=1)

    qf = q_ref[...].astype(F32)
    parts = []
    for j in range(N_HEADS // 2):
        t = qf[:, LANES * j:LANES * (j + 1)].T
        parts += [t[0:HEAD_DIM], t[HEAD_DIM:]]
    qT = jnp.concatenate(parts, axis=1).astype(BF16)
    zeros = jnp.zeros((HEAD_DIM, ROWS), BF16)
    qgT = jnp.concatenate(
        [jnp.concatenate([qT[:, gcols(g)] if gg == g else zeros for gg in range(N_GROUPS)], axis=1)
         for g in range(N_GROUPS)], axis=0)

    step = ncp // CMP_CLASSES
    cls = (b * CMP_PER_Q + CMP_PER_Q - 2) // step
    for k in range(CMP_CLASSES):
        @pl.when(cls == k)
        def _(nk=(k + 1) * step):
            cb = _tile_heads(cbt_ref[pl.ds(pl.multiple_of(ncp - b * CMP_PER_Q, 8), nk), :])
            s = jnp.dot(kc_ref[0:nk, :], qgT, preferred_element_type=F32) + per_group(lambda g: cb)
            m = _colmax(s)
            e = jnp.exp2(s - m)
            inv = jnp.where(m > NEG * 0.5, 1.0 / jnp.maximum(jnp.sum(e, axis=0, keepdims=True), 1e-30), 0.0)
            p = e * inv
            for g in range(N_GROUPS):
                pg = p[:, gcols(g)]
                oc_ref[:, gcols(g)] = jnp.dot(vcT_ref[g, :, 0:nk], pg.astype(BF16), preferred_element_type=F32)
                ps = pg[:, 0:Q_TILE]
                for r in range(1, GQA):
                    ps = ps + pg[:, r * Q_TILE:(r + 1) * Q_TILE]
                ps_ref[g, 0:8, :] = jnp.zeros((8, Q_TILE), F32)
                ps_ref[g, 8:8 + nk, :] = ps
                if nk < ncp:
                    ps_ref[g, 8 + nk:, :] = jnp.zeros((ncp - nk, Q_TILE), F32)

    ratio = SEL_LEN // CMP_STRIDE
    imp = per_group(lambda g: sum(ps_ref[g, pl.ds(8 + d, ns, stride=ratio), :] for d in IMP_TAPS))

    wk = WINDOW + Q_TILE
    ws = pl.multiple_of(jnp.maximum(t0 - WINDOW, 0), Q_TILE)
    wb = _tile_heads(wb_ref[jnp.minimum(b, WIN_EARLY)])
    s = jnp.dot(kwa_ref[pl.ds(ws, wk), :], qgT, preferred_element_type=F32) + per_group(lambda g: wb)
    pw = jnp.exp2(s - _colmax(s)).astype(BF16)
    for g in range(N_GROUPS):
        ow = jnp.dot(vwT_ref[g, :, pl.ds(ws, wk)], pw[:, gcols(g)], preferred_element_type=F32)
        ow_ref[:, gcols(g)] = ow[0:HEAD_DIM] / jnp.maximum(ow[HEAD_DIM:HEAD_DIM + 1], 1e-30)

    lb = _tile_heads(lb_ref[...])
    s = (jnp.dot(ksa_ref[pl.ds(t0, Q_TILE), 0:LANES], qgT, preferred_element_type=F32)
         + per_group(lambda g: lb))
    m0 = _colmax(s)
    p0 = jnp.exp2(s - m0).astype(BF16)
    for g in range(N_GROUPS):
        acc_ref[g] = jnp.dot(vsT_ref[g, :, pl.ds(t0, Q_TILE)], p0[:, gcols(g)], preferred_element_type=F32)
    ma_ref[...] = m0
    ms_ref[...] = m0

    nq = N_GROUPS * Q_TILE
    blk = lax.broadcasted_iota(jnp.int32, (ns, nq), 0)
    cur = (t0 + (lax.broadcasted_iota(jnp.int32, (ns, nq), 1) & (Q_TILE - 1))) // SEL_LEN
    valid = blk <= cur
    forced = valid & ((blk == 0) | (blk == cur) | (blk == cur - 1))
    rank0 = jnp.where(valid & jnp.logical_not(forced), imp, NEG)
    blk_f = blk.astype(F32)
    taken = -(2.0 ** 126)

    def rank(break_ties):
        score = rank0
        for _ in range(N_SELECT - N_FORCED):
            mx = _colmax(score)
            if break_ties:
                idx = jnp.min(jnp.where(score == mx, blk_f, float(ns)), axis=0, keepdims=True)
                hit = blk_f == idx
            else:
                hit = score == mx
            score = jnp.where(hit, taken, score)
        return (score == taken) & (rank0 > NEG * 0.5)

    picked = rank(False)
    count = jnp.sum(jnp.where(picked, 1.0, 0.0), axis=0, keepdims=True)
    want = jnp.clip(cur[0:1] - (N_FORCED - 1), 0, N_SELECT - N_FORCED).astype(F32)
    sel_ref[...] = jnp.where(forced | picked, 1.0, 0.0)

    @pl.when(jnp.logical_not(jnp.all(count == want)))
    def _():
        sel_ref[...] = jnp.where(forced | rank(True), 1.0, 0.0)

    sel = sel_ref[...] > 0.5
    bias = jnp.where(sel & (blk < b * SEL_PER_Q), 0.0, NEG).astype(BF16)
    for w in range(ns // BIAS_WIN):
        qaug_ref[w, 0:2 * HEAD_DIM, :] = qgT
        qaug_ref[w, 2 * HEAD_DIM:, :] = per_group(
            lambda g: _tile_heads(bias[BIAS_WIN * w:BIAS_WIN * (w + 1), gq(g)]))

    n_tiles = jnp.maximum((t0 + KV_TILE - 1) // KV_TILE, 1)
    tiles_per_win = BIAS_WIN * SEL_LEN // KV_TILE

    def scores(t, slot):
        start = pl.multiple_of(t * KV_TILE, KV_TILE)
        s = jnp.dot(ksa_ref[pl.ds(start, KV_TILE), :], qaug_ref[t // tiles_per_win],
                    preferred_element_type=F32)
        sbuf_ref[slot] = s
        ms_ref[...] = jnp.maximum(ms_ref[...], _colmax(s))

    def consume(t, slot, m_s, rows=KV_TILE):
        start = pl.multiple_of(t * KV_TILE, KV_TILE)
        alpha = jnp.exp2(ma_ref[...] - m_s)
        pp = jnp.exp2(sbuf_ref[slot, 0:rows, :] - m_s).astype(BF16)
        for g in range(N_GROUPS):
            acc_ref[g] = alpha[:, gcols(g)] * acc_ref[g] + jnp.dot(
                vsT_ref[g, :, pl.ds(start, rows)], pp[:, gcols(g)], preferred_element_type=F32)
        ma_ref[...] = m_s

    def step_(t, slot):
        m_s = ms_ref[...]
        scores(t + 1, 1 - slot)
        consume(t, slot, m_s)

    scores(0, 0)

    def body(i, carry):
        step_(2 * i, 0)
        step_(2 * i + 1, 1)
        return carry

    n_steps = n_tiles - 1
    lax.fori_loop(0, n_steps // 2, body, 0)

    @pl.when(n_steps % 2 == 1)
    def _():
        step_(n_steps - 1, 0)

    last_chunks = jnp.maximum((t0 - n_steps * KV_TILE) // Q_TILE, 1)
    for k in range(1, KV_TILE // Q_TILE + 1):
        @pl.when(last_chunks == k)
        def _(rows=k * Q_TILE):
            consume(n_steps, n_steps % 2, ms_ref[...], rows)

    owT = ow_ref[...]
    osT = per_group(lambda g: acc_ref[g, 0:HEAD_DIM, :] / jnp.maximum(acc_ref[g, HEAD_DIM:HEAD_DIM + 1, :], 1e-30))

    gall = gate_ref[...].T
    row = lambda i: gall[i:i + 1]
    gt = lambda br: jnp.concatenate([row(g * GATE_SLOT + br * GQA + r)
                                     for g in range(N_GROUPS) for r in range(GQA)], axis=1)
    oT = gt(0) * oc_ref[...] + gt(1) * osT + gt(2) * owT
    for j in range(N_HEADS // 2):
        pair = jnp.concatenate([oT[:, Q_TILE * 2 * j:Q_TILE * (2 * j + 1)],
                                oT[:, Q_TILE * (2 * j + 1):Q_TILE * (2 * j + 2)]], axis=0)
        out_ref[:, LANES * j:LANES * (j + 1)] = pair.T


def _attention(q, gates, ksa, vsT, kwa, vwT, kc, vcT, cbt, wb, lb):
    S = q.shape[0]
    nqb = S // Q_TILE
    ncp = kc.shape[0]
    ns = S // SEL_LEN
    assert KV_WIDTH == LANES
    row = lambda b: (b, 0)
    fix = lambda b: (0, 0)
    fix3 = lambda b: (0, 0, 0)
    kernel = functools.partial(_attn_kernel, ns=ns, ncp=ncp)
    return pl.pallas_call(
        kernel,
        grid=(nqb,),
        in_specs=[pl.BlockSpec((Q_TILE, ATTN_WIDTH), row), pl.BlockSpec((Q_TILE, LANES), row),
                  _resident((S, 2 * LANES), fix), _resident((N_GROUPS, V_ROWS, S), fix3),
                  _resident((S, LANES), fix), _resident((N_GROUPS, V_ROWS, S), fix3),
                  _resident((ncp, LANES), fix), _resident((N_GROUPS, HEAD_DIM, ncp), fix3),
                  _resident((2 * ncp, Q_TILE), fix),
                  _resident((WIN_EARLY + 1, WINDOW + Q_TILE, Q_TILE), fix3),
                  _resident((Q_TILE, Q_TILE), fix)],
        out_specs=pl.BlockSpec((Q_TILE, ATTN_WIDTH), row),
        out_shape=jax.ShapeDtypeStruct((S, ATTN_WIDTH), F32),
        scratch_shapes=[pltpu.VMEM((ns // BIAS_WIN, 2 * LANES, N_COLS), BF16),
                        pltpu.VMEM((2, KV_TILE, N_COLS), F32),
                        pltpu.VMEM((N_GROUPS, V_ROWS, ROWS), F32),
                        pltpu.VMEM((1, N_COLS), F32), pltpu.VMEM((1, N_COLS), F32),
                        pltpu.VMEM((N_GROUPS, 8 + ncp, Q_TILE), F32),
                        pltpu.VMEM((HEAD_DIM, N_COLS), F32),
                        pltpu.VMEM((HEAD_DIM, N_COLS), F32),
                        pltpu.VMEM((ns, N_GROUPS * Q_TILE), F32)],
        compiler_params=pltpu.CompilerParams(dimension_semantics=("arbitrary",), vmem_limit_bytes=VMEM_LIMIT),
        name="nsa_attention",
    )(q, gates, ksa, vsT, kwa, vwT, kc, vcT, cbt, wb, lb)


def _gmlp_kernel(u_ref, v_ref, ws_ref, bs_ref, g_ref, out_ref, wsm_ref, *, chunks):
    @pl.when(pl.program_id(0) == 0)
    def _():
        r = lax.broadcasted_iota(jnp.int32, (CHUNK, CHUNK), 0)
        c = lax.broadcasted_iota(jnp.int32, (CHUNK, CHUNK), 1)
        for g in range(MLP_GROUPS):
            wsm_ref[g] = jnp.where(c <= r, ws_ref[g], 0.0).astype(BF16)

    lane = lax.broadcasted_iota(jnp.int32, (CHUNK, LANES), 1)
    left = lane < HEAD_DIM
    for ci in range(chunks):
        rows = slice(ci * CHUNK, (ci + 1) * CHUNK)
        parts = []
        for pr in range(MLP_GROUPS // 2):
            vp = v_ref[rows, pr * LANES:(pr + 1) * LANES]
            va = jnp.where(left, vp, jnp.zeros_like(vp))
            vb = jnp.where(left, jnp.zeros_like(vp), vp)
            parts.append(jnp.dot(wsm_ref[2 * pr], va, preferred_element_type=F32)
                         + jnp.dot(wsm_ref[2 * pr + 1], vb, preferred_element_type=F32))
        mixed = jnp.concatenate(parts, axis=1) + bs_ref[...]
        y = u_ref[rows, :].astype(F32) * mixed
        out_ref[rows, :] = _rms(y, g_ref[...]).astype(BF16)


def _gmlp(u, v, ws, bs_exp, gain, chunks=8):
    S = u.shape[0]
    tm = CHUNK * chunks
    row = lambda i: (i, 0)
    return pl.pallas_call(
        functools.partial(_gmlp_kernel, chunks=chunks),
        grid=(S // tm,),
        in_specs=[pl.BlockSpec((tm, MLP_WIDTH), row), pl.BlockSpec((tm, MLP_WIDTH), row),
                  _resident((MLP_GROUPS, CHUNK, CHUNK), lambda i: (0, 0, 0)),
                  _resident((CHUNK, MLP_WIDTH), lambda i: (0, 0)),
                  _resident((1, MLP_WIDTH), lambda i: (0, 0))],
        out_specs=pl.BlockSpec((tm, MLP_WIDTH), row),
        out_shape=jax.ShapeDtypeStruct((S, MLP_WIDTH), BF16),
        scratch_shapes=[pltpu.VMEM((MLP_GROUPS, CHUNK, CHUNK), BF16)],
        compiler_params=pltpu.CompilerParams(dimension_semantics=("arbitrary",), vmem_limit_bytes=VMEM_LIMIT),
        name="gmlp",
    )(u, v, ws, bs_exp, gain)


def _outproj_kernel(h_ref, a_ref, m_ref, ag_ref, wa_ref, wm_ref, pg_ref, out_ref):
    an = _rms(a_ref[...], ag_ref[...]).astype(BF16)
    mix = (jnp.dot(an, wa_ref[...], preferred_element_type=F32)
           + jnp.dot(m_ref[...], wm_ref[...], preferred_element_type=F32))
    out_ref[...] = h_ref[...] + _rms(mix, pg_ref[...])


def _outproj(h, attn, mlpn, ag, wa, wm, pg, tm=1024):
    S, D = h.shape
    row = lambda i: (i, 0)
    fix = lambda i: (0, 0)
    return pl.pallas_call(
        _outproj_kernel,
        grid=(S // tm,),
        in_specs=[pl.BlockSpec((tm, D), row), pl.BlockSpec((tm, ATTN_WIDTH), row), pl.BlockSpec((tm, MLP_WIDTH), row),
                  _resident((1, ATTN_WIDTH), fix), _resident((ATTN_WIDTH, D), fix),
                  _resident((MLP_WIDTH, D), fix), _resident((1, D), fix)],
        out_specs=pl.BlockSpec((tm, D), row),
        out_shape=jax.ShapeDtypeStruct((S, D), F32),
        compiler_params=pltpu.CompilerParams(dimension_semantics=("arbitrary",), vmem_limit_bytes=VMEM_LIMIT),
        name="outproj",
    )(h, attn, mlpn, ag, wa, wm, pg)


HALO = BF16_ROWS


def _ffn_kernel(h_ref, halo_ref, p_ref, g_ref, wup_ref, cw_ref, cb_ref, wd_ref, pg_ref, plg_ref, wpg_ref, wpp_ref,
                out_ref, acc_ref, *, fc):
    dff = wd_ref.shape[0]
    h = h_ref[...]
    hn = _rms(halo_ref[...], g_ref[...])
    hn = jnp.where(pl.program_id(0) == 0, 0.0, hn)
    xn = jnp.concatenate([hn.astype(BF16), _rms(h, g_ref[...]).astype(BF16)], axis=0)

    def conv(lo, n):
        hh = jnp.dot(xn, wup_ref[:, lo:lo + n], preferred_element_type=F32)
        cw = cw_ref[:, lo:lo + n]
        y = cb_ref[:, lo:lo + n] + pltpu.roll(hh, 2, 0) * cw[0:1] + pltpu.roll(hh, 1, 0) * cw[1:2] + hh * cw[2:3]
        return y[HALO:, :]

    for c0 in range(0, dff, fc):
        n = min(fc, dff - c0)
        act = (jax.nn.silu(conv(c0, n)) * conv(dff + c0, n)).astype(BF16)
        part = jnp.dot(act, wd_ref[c0:c0 + n, :], preferred_element_type=F32)
        if c0 == 0:
            acc_ref[...] = part
        else:
            acc_ref[...] += part

    h2 = h + _rms(acc_ref[...], pg_ref[...])
    gate = jax.nn.sigmoid(jnp.dot(_rms(h2, plg_ref[...]).astype(BF16), wpg_ref[...], preferred_element_type=F32))
    proj = jnp.dot(p_ref[...].astype(BF16), wpp_ref[...], preferred_element_type=F32)
    out_ref[...] = h2 + gate * proj


def _ffn(h, p, gain, w_up, conv_w, conv_b, w_down, pg, plg, wpg, wpp, tm=1024, fc=1024):
    S, D = h.shape
    dff = w_down.shape[0]
    P = p.shape[1]
    row = lambda i: (i, 0)
    fix = lambda i: (0, 0)
    halo = lambda i: (jnp.maximum(i * (tm // HALO) - 1, 0), 0)
    return pl.pallas_call(
        functools.partial(_ffn_kernel, fc=fc),
        grid=(S // tm,),
        in_specs=[pl.BlockSpec((tm, D), row), pl.BlockSpec((HALO, D), halo), pl.BlockSpec((tm, P), row),
                  _resident((1, D), fix), _resident((D, 2 * dff), fix), _resident((CONV_WIDTH, 2 * dff), fix),
                  _resident((1, 2 * dff), fix), _resident((dff, D), fix), _resident((1, D), fix),
                  _resident((1, D), fix), _resident((D, D), fix), _resident((P, D), fix)],
        out_specs=pl.BlockSpec((tm, D), row),
        out_shape=jax.ShapeDtypeStruct((S, D), F32),
        scratch_shapes=[pltpu.VMEM((tm, D), F32)],
        compiler_params=pltpu.CompilerParams(dimension_semantics=("arbitrary",), vmem_limit_bytes=VMEM_LIMIT),
        name="convffn_ple",
    )(h, h, p, gain, w_up, conv_w, conv_b, w_down, pg, plg, wpg, wpp)


def _rope_tables(pos):
    half = ROT_DIM // 2
    inv = ROPE_THETA ** (-np.arange(half, dtype=np.float64) / half)
    ang = np.asarray(pos, np.float64)[:, None] * inv[None, :]
    cos, sin = jnp.asarray(np.cos(ang), dtype=F32), jnp.asarray(np.sin(ang), dtype=F32)
    n = ang.shape[0]
    one = jnp.ones((n, HEAD_DIM - ROT_DIM), F32)
    zero = jnp.zeros((n, HEAD_DIM - ROT_DIM), F32)
    zh = jnp.zeros((n, half), F32)
    c = jnp.concatenate([cos, cos, one], axis=1)
    sm = jnp.concatenate([-sin, zh, zero], axis=1)
    sp = jnp.concatenate([zh, sin, zero], axis=1)
    dup = lambda t: jnp.concatenate([t, t], axis=1)
    return dup(c), dup(sm), dup(sp)


def _mask_tables(ncp):
    qo = np.arange(Q_TILE)[None, :]
    rel = (np.arange(2 * ncp) - ncp)[:, None]
    cmp_ok = rel * CMP_STRIDE + CMP_LEN - 1 <= qo
    r = np.arange(WINDOW + Q_TILE)[:, None]
    win_ok = []
    for v in range(WIN_EARLY):
        tq = v * Q_TILE + qo
        win_ok.append((r <= tq) & (r > tq - WINDOW))
    win_ok.append((r - WINDOW <= qo) & (r > qo))
    loc_ok = np.arange(Q_TILE)[:, None] <= qo
    tab = lambda ok: jnp.asarray(np.where(ok, 0.0, NEG), dtype=F32)
    return tab(cmp_ok), tab(np.stack(win_ok)), tab(loc_ok)


def _gate_columns():
    src = np.full((LANES,), -1)
    for g in range(N_GROUPS):
        for br in range(N_BRANCH):
            for r in range(GQA):
                src[g * GATE_SLOT + br * GQA + r] = (g * GQA + r) * N_BRANCH + br
    return src


def _layer(h, p, prm, tabs):
    S, D = h.shape
    rc, rm, rp, crc, crm, crp, cbt, wb, lb = tabs

    sizes = [ATTN_WIDTH] + [KV_WIDTH] * 6 + [N_HEADS * N_BRANCH, MLP_WIDTH, MLP_WIDTH]
    offs = np.concatenate([[0], np.cumsum(sizes)])
    w_in = prm["w_in"]
    seg = lambda k: w_in[:, offs[k]:offs[k + 1]]
    src = _gate_columns()
    wg = jnp.where(jnp.asarray(src >= 0)[None, :], seg(7)[:, np.maximum(src, 0)], 0.0)
    w_cat = jnp.concatenate([seg(0)] + [seg(k) for k in range(1, 7)] + [seg(8), seg(9), wg], axis=1).astype(BF16)

    q, ksa, kwa, vsT, vwT, kcx, vcx, u, v, gates = _inproj(h, prm["pre_mix_g"][None], w_cat, rc, rm, rp,
                                                           prm["gmlp_ln_g"][None], prm["gmlp_ln_b"][None])

    w1 = prm["cmp_w1"].astype(BF16)
    half = CMP_STRIDE * HEAD_DIM
    w1l = jnp.concatenate([w1[:, :half], w1[:, half:]], axis=2).reshape(2, CMP_STRIDE, HEAD_DIM, 2 * CMP_HIDDEN)
    zl = jnp.zeros_like(w1l)
    wboth = jnp.concatenate([jnp.concatenate([w1l, zl], axis=3), jnp.concatenate([zl, w1l], axis=3)], axis=2)
    wboth = wboth.reshape(2, CMP_STRIDE * LANES, 2 * N_GROUPS * CMP_HIDDEN)
    pe = jnp.broadcast_to(prm["cmp_pe"].reshape(2, 1, CMP_LEN * HEAD_DIM), (2, 8, CMP_LEN * HEAD_DIM)).astype(BF16)
    w2 = jnp.pad(prm["cmp_w2"], ((0, 0), (0, 0), (0, LANES - HEAD_DIM))).astype(BF16)
    b2 = jnp.pad(prm["cmp_b2"], ((0, 0), (0, LANES - HEAD_DIM)))[:, None]
    kc, vcT = _compress(kcx, vcx, wboth, w1, pe, prm["cmp_b1"][:, None], w2, b2, crc, crm, crp)

    attn = _attention(q, gates, ksa, vsT, kwa, vwT, kc, vcT, cbt, wb, lb)

    bs_exp = jnp.repeat(prm["gmlp_bs"].T, HEAD_DIM, axis=1)
    mlpn = _gmlp(u, v, prm["gmlp_ws"], bs_exp, prm["mlp_out_g"][None])

    w_o = prm["w_o"].astype(BF16)
    h = _outproj(h, attn, mlpn, prm["attn_out_g"][None], w_o[:ATTN_WIDTH], w_o[ATTN_WIDTH:], prm["post_mix_g"][None])
    h = _ffn(h, p, prm["pre_ffn_g"][None], prm["w_up"].astype(BF16), prm["conv_w"], prm["conv_b"][None],
             prm["w_down"].astype(BF16), prm["post_ffn_g"][None], prm["ple_norm_g"][None],
             prm["w_ple_gate"].astype(BF16), prm["w_ple_proj"].astype(BF16))
    return h


def kernel(x, p, pre_mix_g, w_in, cmp_pe, cmp_w1, cmp_b1, cmp_w2, cmp_b2, gmlp_ln_g, gmlp_ln_b, gmlp_ws, gmlp_bs,
           attn_out_g, mlp_out_g, w_o, post_mix_g, pre_ffn_g, w_up, conv_w, conv_b, w_down, post_ffn_g,
           ple_norm_g, w_ple_gate, w_ple_proj):
    B, S, D = x.shape
    depth = p.shape[0]
    assert S % (BIAS_WIN * SEL_LEN) == 0 and D % LANES == 0
    nch = S // CMP_STRIDE
    rc, rm, rp = _rope_tables(np.arange(S))
    crc, crm, crp = _rope_tables(np.arange(nch) * CMP_STRIDE + CMP_LEN - 1)
    tabs = (rc, rm, rp, crc, crm, crp) + _mask_tables(nch)
    stacked = dict(pre_mix_g=pre_mix_g, w_in=w_in, cmp_pe=cmp_pe, cmp_w1=cmp_w1, cmp_b1=cmp_b1, cmp_w2=cmp_w2,
                   cmp_b2=cmp_b2, gmlp_ln_g=gmlp_ln_g, gmlp_ln_b=gmlp_ln_b, gmlp_ws=gmlp_ws, gmlp_bs=gmlp_bs,
                   attn_out_g=attn_out_g, mlp_out_g=mlp_out_g, w_o=w_o, post_mix_g=post_mix_g, pre_ffn_g=pre_ffn_g,
                   w_up=w_up, conv_w=conv_w, conv_b=conv_b, w_down=w_down, post_ffn_g=post_ffn_g,
                   ple_norm_g=ple_norm_g, w_ple_gate=w_ple_gate, w_ple_proj=w_ple_proj)
    outs = []
    for bi in range(B):
        h = x[bi]
        for i in range(depth):
            h = _layer(h, p[i, bi], {k: v[i] for k, v in stacked.items()}, tabs)
        outs.append(h)
    return jnp.stack(outs)
```

```python
import functools
import math

import numpy as np
import jax
import jax.numpy as jnp
from jax import lax
from jax.experimental import pallas as pl
from jax.experimental.pallas import tpu as pltpu

F32 = jnp.float32
BF16 = jnp.bfloat16

HEAD_DIM = 64
N_HEADS = 8
N_GROUPS = 2
GQA = N_HEADS // N_GROUPS
N_BRANCH = 3
ATTN_WIDTH = N_HEADS * HEAD_DIM
MLP_GROUPS = 8
MLP_WIDTH = MLP_GROUPS * HEAD_DIM
KV_WIDTH = N_GROUPS * HEAD_DIM
ROT_DIM = HEAD_DIM // 4
ROPE_THETA = 500000.0
CMP_LEN = 32
CMP_STRIDE = 16
CMP_HIDDEN = 256
SEL_LEN = 64
N_SELECT = 16
WINDOW = 512
CHUNK = 128
CONV_WIDTH = 3
NORM_EPS = 1e-6
NEG = -1e30
SCALE = HEAD_DIM ** -0.5
QSCALE = SCALE * math.log2(math.e)

LANES = 128
SUBLANES = 8
BF16_ROWS = 16
Q_TILE = 128
ROWS = GQA * Q_TILE
N_COLS = N_GROUPS * ROWS
KV_TILE = 1024
BIAS_WIN = 128
V_ROWS = HEAD_DIM + BF16_ROWS
CMP_PER_Q = Q_TILE // CMP_STRIDE
SEL_PER_Q = Q_TILE // SEL_LEN
CMP_CLASSES = 8
N_FORCED = 3
GATE_SLOT = 16
WIN_EARLY = WINDOW // Q_TILE
IMP_TAPS = range(-1, SEL_LEN // CMP_STRIDE)
PS_PAD = SUBLANES
VMEM_LIMIT = 56 * 1024 * 1024


def _rms(x, gain):
    return x * lax.rsqrt(jnp.mean(x * x, axis=-1, keepdims=True) + NORM_EPS) * gain


def _rope_slab(x, c, sm, sp):
    return x * c + pltpu.roll(x, LANES - ROT_DIM // 2, 1) * sm + pltpu.roll(x, ROT_DIM // 2, 1) * sp


def _colmax(s):
    return jnp.max(s, axis=0, keepdims=True)


def _tile_heads(x):
    return jnp.concatenate([x] * GQA, axis=1)


def _resident(shape, index_map):
    return pl.BlockSpec(shape, index_map, pipeline_mode=pl.Buffered(1))


def _inproj_kernel(x_ref, g_ref, w_ref, rc_ref, rm_ref, rp_ref, lng_ref, lnb_ref,
                   q_ref, ksa_ref, kwa_ref, vsT_ref, vwT_ref, kcx_ref, vcx_ref, u_ref, v_ref, gate_ref, craw_ref):
    tm = x_ref.shape[0]
    a = _rms(x_ref[...], g_ref[...]).astype(BF16)
    c, sm, sp = rc_ref[...], rm_ref[...], rp_ref[...]
    zq = jnp.dot(a, w_ref[:, 0:ATTN_WIDTH], preferred_element_type=F32)
    for j in range(ATTN_WIDTH // LANES):
        sl = slice(LANES * j, LANES * (j + 1))
        q_ref[:, sl] = (_rope_slab(zq[:, sl], c, sm, sp) * QSCALE).astype(BF16)
    o = ATTN_WIDTH
    zkv = jnp.dot(a, w_ref[:, o:o + 6 * KV_WIDTH], preferred_element_type=F32)
    slab = lambda j: zkv[:, LANES * j:LANES * (j + 1)]

    craw_ref[0] = slab(0)
    craw_ref[1] = slab(1)
    nch = tm // CMP_STRIDE
    for l in range(CMP_STRIDE):
        kcx_ref[l] = craw_ref[0, pl.ds(l, nch, stride=CMP_STRIDE), :].astype(BF16)
        vcx_ref[l] = craw_ref[1, pl.ds(l, nch, stride=CMP_STRIDE), :].astype(BF16)

    pos = pl.program_id(0) * tm + lax.broadcasted_iota(jnp.int32, (tm, BIAS_WIN), 0)
    lane = lax.broadcasted_iota(jnp.int32, (tm, BIAS_WIN), 1)
    ksa_ref[:, 0:LANES] = _rope_slab(slab(2), c, sm, sp).astype(BF16)
    ksa_ref[:, LANES:] = jnp.where((pos // SEL_LEN) % BIAS_WIN == lane, 1.0, 0.0).astype(BF16)
    kwa_ref[...] = _rope_slab(slab(4), c, sm, sp).astype(BF16)
    for k in range(tm // LANES):
        rs = slice(LANES * k, LANES * (k + 1))
        for ref, sl in ((vsT_ref, slab(3)), (vwT_ref, slab(5))):
            t = sl[rs, :].T.astype(BF16)
            for g in range(N_GROUPS):
                ref[g, 0:HEAD_DIM, rs] = t[HEAD_DIM * g:HEAD_DIM * (g + 1)]
                ref[g, HEAD_DIM:, rs] = jnp.ones((BF16_ROWS, LANES), BF16)

    o += 6 * KV_WIDTH
    zu = jnp.dot(a, w_ref[:, o:o + MLP_WIDTH], preferred_element_type=F32)
    u_ref[...] = jax.nn.gelu(zu).astype(BF16)
    o += MLP_WIDTH
    zv = jax.nn.gelu(jnp.dot(a, w_ref[:, o:o + MLP_WIDTH], preferred_element_type=F32))
    mu = jnp.mean(zv, axis=-1, keepdims=True)
    d = zv - mu
    var = jnp.mean(d * d, axis=-1, keepdims=True)
    v_ref[...] = (d * lax.rsqrt(var + NORM_EPS) * lng_ref[...] + lnb_ref[...]).astype(BF16)
    o += MLP_WIDTH
    zg = jnp.dot(a, w_ref[:, o:o + LANES], preferred_element_type=F32)
    gate_ref[...] = jax.nn.sigmoid(zg)


def _inproj(h, gain, w, rc, rm, rp, lng, lnb, tm=1024):
    S, D = h.shape
    N = w.shape[1]
    nch = S // CMP_STRIDE
    row = lambda i: (i, 0)
    fix = lambda i: (0, 0)
    chunked = lambda i: (0, i, 0)
    bf = lambda *shape: jax.ShapeDtypeStruct(shape, BF16)
    return pl.pallas_call(
        _inproj_kernel,
        grid=(S // tm,),
        in_specs=[pl.BlockSpec((tm, D), row), _resident((1, D), fix), _resident((D, N), fix),
                  pl.BlockSpec((tm, LANES), row), pl.BlockSpec((tm, LANES), row), pl.BlockSpec((tm, LANES), row),
                  _resident((1, MLP_WIDTH), fix), _resident((1, MLP_WIDTH), fix)],
        out_specs=[pl.BlockSpec((tm, ATTN_WIDTH), row), pl.BlockSpec((tm, 2 * LANES), row),
                   pl.BlockSpec((tm, LANES), row), pl.BlockSpec((N_GROUPS, V_ROWS, tm), lambda i: (0, 0, i)),
                   pl.BlockSpec((N_GROUPS, V_ROWS, tm), lambda i: (0, 0, i)),
                   pl.BlockSpec((CMP_STRIDE, tm // CMP_STRIDE, LANES), chunked),
                   pl.BlockSpec((CMP_STRIDE, tm // CMP_STRIDE, LANES), chunked),
                   pl.BlockSpec((tm, MLP_WIDTH), row), pl.BlockSpec((tm, MLP_WIDTH), row),
                   pl.BlockSpec((tm, LANES), row)],
        out_shape=[bf(S, ATTN_WIDTH), bf(S, 2 * LANES), bf(S, LANES), bf(N_GROUPS, V_ROWS, S), bf(N_GROUPS, V_ROWS, S),
                   bf(CMP_STRIDE, nch, LANES), bf(CMP_STRIDE, nch, LANES), bf(S, MLP_WIDTH), bf(S, MLP_WIDTH),
                   jax.ShapeDtypeStruct((S, LANES), F32)],
        scratch_shapes=[pltpu.VMEM((2, tm, LANES), F32)],
        compiler_params=pltpu.CompilerParams(dimension_semantics=("arbitrary",), vmem_limit_bytes=VMEM_LIMIT),
        name="inproj",
    )(h, gain, w, rc, rm, rp, lng, lnb)


def _compress_kernel(xk_ref, xv_ref, wb_ref, w1_ref, pe_ref, b1_ref, w2_ref, b2_ref, rc_ref, rm_ref, rp_ref,
                     kc_ref, vcT_ref):
    nch = xk_ref.shape[1]

    def run(x_ref):
        x = jnp.concatenate([x_ref[l] for l in range(CMP_STRIDE)], axis=1)
        c = jnp.dot(x, wb_ref[0], preferred_element_type=F32)
        cvec = jnp.dot(pe_ref[0], w1_ref[0], preferred_element_type=F32)[0:1] + b1_ref[0]
        outs = []
        for g in range(N_GROUPS):
            cg = c[:, 2 * CMP_HIDDEN * g:2 * CMP_HIDDEN * (g + 1)]
            h = jax.nn.gelu(cg[:, :CMP_HIDDEN] + pltpu.roll(cg[:, CMP_HIDDEN:], nch - 1, 0) + cvec)
            outs.append(jnp.dot(h.astype(BF16), w2_ref[0], preferred_element_type=F32) + b2_ref[0])
        return outs

    @pl.when(pl.program_id(0) == 0)
    def _():
        o = run(xk_ref)
        shared = o[0]
        for g in range(1, N_GROUPS):
            shared = shared + pltpu.roll(o[g], HEAD_DIM * g, 1)
        kc_ref[...] = _rope_slab(shared, rc_ref[...], rm_ref[...], rp_ref[...]).astype(BF16)

    @pl.when(pl.program_id(0) == 1)
    def _():
        for g, o in enumerate(run(xv_ref)):
            for k in range(nch // LANES):
                rs = slice(LANES * k, LANES * (k + 1))
                vcT_ref[g, :, rs] = o[rs, :].T[0:HEAD_DIM].astype(BF16)


def _compress(kcx, vcx, wboth, w1, pe, b1, w2, b2, rc, rm, rp):
    _, nch, _ = kcx.shape
    cw = CMP_STRIDE * HEAD_DIM
    kv3 = lambda k: (k, 0, 0)
    fix3 = lambda k: (0, 0, 0)
    fix2 = lambda k: (0, 0)
    return pl.pallas_call(
        _compress_kernel,
        grid=(2,),
        in_specs=[_resident((CMP_STRIDE, nch, LANES), fix3), _resident((CMP_STRIDE, nch, LANES), fix3),
                  pl.BlockSpec((1, CMP_STRIDE * LANES, 2 * N_GROUPS * CMP_HIDDEN), kv3),
                  pl.BlockSpec((1, 2 * cw, CMP_HIDDEN), kv3), pl.BlockSpec((1, SUBLANES, 2 * cw), kv3),
                  pl.BlockSpec((1, 1, CMP_HIDDEN), kv3), pl.BlockSpec((1, CMP_HIDDEN, LANES), kv3),
                  pl.BlockSpec((1, 1, LANES), kv3),
                  _resident((nch, LANES), fix2), _resident((nch, LANES), fix2), _resident((nch, LANES), fix2)],
        out_specs=[pl.BlockSpec((nch, LANES), fix2), pl.BlockSpec((N_GROUPS, HEAD_DIM, nch), fix3)],
        out_shape=[jax.ShapeDtypeStruct((nch, LANES), BF16),
                   jax.ShapeDtypeStruct((N_GROUPS, HEAD_DIM, nch), BF16)],
        compiler_params=pltpu.CompilerParams(dimension_semantics=("arbitrary",), vmem_limit_bytes=VMEM_LIMIT),
        name="compress",
    )(kcx, vcx, wboth, w1, pe, b1, w2, b2, rc, rm, rp)


def _attn_kernel(q_ref, gate_ref, ksa_ref, vsT_ref, kwa_ref, vwT_ref, kc_ref, vcT_ref, cbt_ref, wb_ref, lb_ref,
                 out_ref, qaug_ref, sbuf_ref, acc_ref, ma_ref, ms_ref, ps_ref, oc_ref, ow_ref, sel_ref,
                 *, ns, ncp):
    b = pl.program_id(0)
    t0 = pl.multiple_of(b * Q_TILE, Q_TILE)
    gcols = lambda g: slice(ROWS * g, ROWS * (g + 1))
    gq = lambda g: slice(Q_TILE * g, Q_TILE * (g + 1))
    per_group = lambda f: jnp.concatenate([f(g) for g in range(N_GROUPS)], axis=1)

    qf = q_ref[...].astype(F32)
    parts = []
    for j in range(N_HEADS // 2):
        t = qf[:, LANES * j:LANES * (j + 1)].T
        parts += [t[0:HEAD_DIM], t[HEAD_DIM:]]
    qT = jnp.concatenate(parts, axis=1).astype(BF16)
    zeros = jnp.zeros((HEAD_DIM, ROWS), BF16)
    qgT = jnp.concatenate(
        [jnp.concatenate([qT[:, gcols(g)] if gg == g else zeros for gg in range(N_GROUPS)], axis=1)
         for g in range(N_GROUPS)], axis=0)

    step = ncp // CMP_CLASSES
    cls = (b * CMP_PER_Q + CMP_PER_Q - 2) // step
    for k in range(CMP_CLASSES):
        @pl.when(cls == k)
        def _(nk=(k + 1) * step):
            cb = _tile_heads(cbt_ref[pl.ds(pl.multiple_of(ncp - b * CMP_PER_Q, SUBLANES), nk), :])
            s = jnp.dot(kc_ref[0:nk, :], qgT, preferred_element_type=F32) + per_group(lambda g: cb)
            m = _colmax(s)
            e = jnp.exp2(s - m)
            inv = jnp.where(m > NEG * 0.5, 1.0 / jnp.maximum(jnp.sum(e, axis=0, keepdims=True), 1e-30), 0.0)
            p = e * inv
            for g in range(N_GROUPS):
                pg = p[:, gcols(g)]
                oc_ref[:, gcols(g)] = jnp.dot(vcT_ref[g, :, 0:nk], pg.astype(BF16), preferred_element_type=F32)
                ps = pg[:, 0:Q_TILE]
                for r in range(1, GQA):
                    ps = ps + pg[:, r * Q_TILE:(r + 1) * Q_TILE]
                ps_ref[g, 0:PS_PAD, :] = jnp.zeros((PS_PAD, Q_TILE), F32)
                ps_ref[g, PS_PAD:PS_PAD + nk, :] = ps
                if nk < ncp:
                    ps_ref[g, PS_PAD + nk:, :] = jnp.zeros((ncp - nk, Q_TILE), F32)

    ratio = SEL_LEN // CMP_STRIDE
    imp = per_group(lambda g: sum(ps_ref[g, pl.ds(PS_PAD + d, ns, stride=ratio), :] for d in IMP_TAPS))

    wk = WINDOW + Q_TILE
    ws = pl.multiple_of(jnp.maximum(t0 - WINDOW, 0), Q_TILE)
    wb = _tile_heads(wb_ref[jnp.minimum(b, WIN_EARLY)])
    s = jnp.dot(kwa_ref[pl.ds(ws, wk), :], qgT, preferred_element_type=F32) + per_group(lambda g: wb)
    pw = jnp.exp2(s - _colmax(s)).astype(BF16)
    for g in range(N_GROUPS):
        ow = jnp.dot(vwT_ref[g, :, pl.ds(ws, wk)], pw[:, gcols(g)], preferred_element_type=F32)
        ow_ref[:, gcols(g)] = ow[0:HEAD_DIM] / jnp.maximum(ow[HEAD_DIM:HEAD_DIM + 1], 1e-30)

    lb = _tile_heads(lb_ref[...])
    s = (jnp.dot(ksa_ref[pl.ds(t0, Q_TILE), 0:LANES], qgT, preferred_element_type=F32)
         + per_group(lambda g: lb))
    m0 = _colmax(s)
    p0 = jnp.exp2(s - m0).astype(BF16)
    for g in range(N_GROUPS):
        acc_ref[g] = jnp.dot(vsT_ref[g, :, pl.ds(t0, Q_TILE)], p0[:, gcols(g)], preferred_element_type=F32)
    ma_ref[...] = m0
    ms_ref[...] = m0

    nq = N_GROUPS * Q_TILE
    blk = lax.broadcasted_iota(jnp.int32, (ns, nq), 0)
    cur = (t0 + (lax.broadcasted_iota(jnp.int32, (ns, nq), 1) & (Q_TILE - 1))) // SEL_LEN
    valid = blk <= cur
    forced = valid & ((blk == 0) | (blk == cur) | (blk == cur - 1))
    rank0 = jnp.where(valid & jnp.logical_not(forced), imp, NEG)
    blk_f = blk.astype(F32)
    taken = -(2.0 ** 126)

    def rank(break_ties):
        score = rank0
        for _ in range(N_SELECT - N_FORCED):
            mx = _colmax(score)
            if break_ties:
                idx = jnp.min(jnp.where(score == mx, blk_f, float(ns)), axis=0, keepdims=True)
                hit = blk_f == idx
            else:
                hit = score == mx
            score = jnp.where(hit, taken, score)
        return (score == taken) & (rank0 > NEG * 0.5)

    picked = rank(False)
    count = jnp.sum(jnp.where(picked, 1.0, 0.0), axis=0, keepdims=True)
    want = jnp.clip(cur[0:1] - (N_FORCED - 1), 0, N_SELECT - N_FORCED).astype(F32)
    sel_ref[...] = jnp.where(forced | picked, 1.0, 0.0)

    @pl.when(jnp.logical_not(jnp.all(count == want)))
    def _():
        sel_ref[...] = jnp.where(forced | rank(True), 1.0, 0.0)

    sel = sel_ref[...] > 0.5
    bias = jnp.where(sel & (blk < b * SEL_PER_Q), 0.0, NEG).astype(BF16)
    for w in range(ns // BIAS_WIN):
        qaug_ref[w, 0:2 * HEAD_DIM, :] = qgT
        qaug_ref[w, 2 * HEAD_DIM:, :] = per_group(
            lambda g: _tile_heads(bias[BIAS_WIN * w:BIAS_WIN * (w + 1), gq(g)]))

    n_tiles = jnp.maximum((t0 + KV_TILE - 1) // KV_TILE, 1)
    tiles_per_win = BIAS_WIN * SEL_LEN // KV_TILE

    def scores(t, slot):
        start = pl.multiple_of(t * KV_TILE, KV_TILE)
        s = jnp.dot(ksa_ref[pl.ds(start, KV_TILE), :], qaug_ref[t // tiles_per_win],
                    preferred_element_type=F32)
        sbuf_ref[slot] = s
        ms_ref[...] = jnp.maximum(ms_ref[...], _colmax(s))

    def consume(t, slot, m_s, rows=KV_TILE):
        start = pl.multiple_of(t * KV_TILE, KV_TILE)
        alpha = jnp.exp2(ma_ref[...] - m_s)
        pp = jnp.exp2(sbuf_ref[slot, 0:rows, :] - m_s).astype(BF16)
        for g in range(N_GROUPS):
            acc_ref[g] = alpha[:, gcols(g)] * acc_ref[g] + jnp.dot(
                vsT_ref[g, :, pl.ds(start, rows)], pp[:, gcols(g)], preferred_element_type=F32)
        ma_ref[...] = m_s

    def step_(t, slot):
        m_s = ms_ref[...]
        scores(t + 1, 1 - slot)
        consume(t, slot, m_s)

    scores(0, 0)

    def body(i, carry):
        step_(2 * i, 0)
        step_(2 * i + 1, 1)
        return carry

    n_steps = n_tiles - 1
    lax.fori_loop(0, n_steps // 2, body, 0)

    @pl.when(n_steps % 2 == 1)
    def _():
        step_(n_steps - 1, 0)

    last_chunks = jnp.maximum((t0 - n_steps * KV_TILE) // Q_TILE, 1)
    for k in range(1, KV_TILE // Q_TILE + 1):
        @pl.when(last_chunks == k)
        def _(rows=k * Q_TILE):
            consume(n_steps, n_steps % 2, ms_ref[...], rows)

    owT = ow_ref[...]
    osT = per_group(lambda g: acc_ref[g, 0:HEAD_DIM, :] / jnp.maximum(acc_ref[g, HEAD_DIM:HEAD_DIM + 1, :], 1e-30))

    gall = gate_ref[...].T
    row = lambda i: gall[i:i + 1]
    gt = lambda br: jnp.concatenate([row(g * GATE_SLOT + br * GQA + r)
                                     for g in range(N_GROUPS) for r in range(GQA)], axis=1)
    oT = gt(0) * oc_ref[...] + gt(1) * osT + gt(2) * owT
    for j in range(N_HEADS // 2):
        pair = jnp.concatenate([oT[:, Q_TILE * 2 * j:Q_TILE * (2 * j + 1)],
                                oT[:, Q_TILE * (2 * j + 1):Q_TILE * (2 * j + 2)]], axis=0)
        out_ref[:, LANES * j:LANES * (j + 1)] = pair.T


def _attention(q, gates, ksa, vsT, kwa, vwT, kc, vcT, cbt, wb, lb):
    S = q.shape[0]
    nqb = S // Q_TILE
    ncp = kc.shape[0]
    ns = S // SEL_LEN
    assert KV_WIDTH == LANES
    row = lambda b: (b, 0)
    fix = lambda b: (0, 0)
    fix3 = lambda b: (0, 0, 0)
    kernel = functools.partial(_attn_kernel, ns=ns, ncp=ncp)
    return pl.pallas_call(
        kernel,
        grid=(nqb,),
        in_specs=[pl.BlockSpec((Q_TILE, ATTN_WIDTH), row), pl.BlockSpec((Q_TILE, LANES), row),
                  _resident((S, 2 * LANES), fix), _resident((N_GROUPS, V_ROWS, S), fix3),
                  _resident((S, LANES), fix), _resident((N_GROUPS, V_ROWS, S), fix3),
                  _resident((ncp, LANES), fix), _resident((N_GROUPS, HEAD_DIM, ncp), fix3),
                  _resident((2 * ncp, Q_TILE), fix),
                  _resident((WIN_EARLY + 1, WINDOW + Q_TILE, Q_TILE), fix3),
                  _resident((Q_TILE, Q_TILE), fix)],
        out_specs=pl.BlockSpec((Q_TILE, ATTN_WIDTH), row),
        out_shape=jax.ShapeDtypeStruct((S, ATTN_WIDTH), F32),
        scratch_shapes=[pltpu.VMEM((ns // BIAS_WIN, 2 * LANES, N_COLS), BF16),
                        pltpu.VMEM((2, KV_TILE, N_COLS), F32),
                        pltpu.VMEM((N_GROUPS, V_ROWS, ROWS), F32),
                        pltpu.VMEM((1, N_COLS), F32), pltpu.VMEM((1, N_COLS), F32),
                        pltpu.VMEM((N_GROUPS, PS_PAD + ncp, Q_TILE), F32),
                        pltpu.VMEM((HEAD_DIM, N_COLS), F32),
                        pltpu.VMEM((HEAD_DIM, N_COLS), F32),
                        pltpu.VMEM((ns, N_GROUPS * Q_TILE), F32)],
        compiler_params=pltpu.CompilerParams(dimension_semantics=("arbitrary",), vmem_limit_bytes=VMEM_LIMIT),
        name="nsa_attention",
    )(q, gates, ksa, vsT, kwa, vwT, kc, vcT, cbt, wb, lb)


def _gmlp_kernel(u_ref, v_ref, ws_ref, bs_ref, g_ref, out_ref, wsm_ref, *, chunks):
    @pl.when(pl.program_id(0) == 0)
    def _():
        r = lax.broadcasted_iota(jnp.int32, (CHUNK, CHUNK), 0)
        c = lax.broadcasted_iota(jnp.int32, (CHUNK, CHUNK), 1)
        for g in range(MLP_GROUPS):
            wsm_ref[g] = jnp.where(c <= r, ws_ref[g], 0.0).astype(BF16)

    lane = lax.broadcasted_iota(jnp.int32, (CHUNK, LANES), 1)
    left = lane < HEAD_DIM
    for ci in range(chunks):
        rows = slice(ci * CHUNK, (ci + 1) * CHUNK)
        parts = []
        for pr in range(MLP_GROUPS // 2):
            vp = v_ref[rows, pr * LANES:(pr + 1) * LANES]
            va = jnp.where(left, vp, jnp.zeros_like(vp))
            vb = jnp.where(left, jnp.zeros_like(vp), vp)
            parts.append(jnp.dot(wsm_ref[2 * pr], va, preferred_element_type=F32)
                         + jnp.dot(wsm_ref[2 * pr + 1], vb, preferred_element_type=F32))
        mixed = jnp.concatenate(parts, axis=1) + bs_ref[...]
        y = u_ref[rows, :].astype(F32) * mixed
        out_ref[rows, :] = _rms(y, g_ref[...]).astype(BF16)


def _gmlp(u, v, ws, bs_exp, gain, chunks=8):
    S = u.shape[0]
    tm = CHUNK * chunks
    row = lambda i: (i, 0)
    return pl.pallas_call(
        functools.partial(_gmlp_kernel, chunks=chunks),
        grid=(S // tm,),
        in_specs=[pl.BlockSpec((tm, MLP_WIDTH), row), pl.BlockSpec((tm, MLP_WIDTH), row),
                  _resident((MLP_GROUPS, CHUNK, CHUNK), lambda i: (0, 0, 0)),
                  _resident((CHUNK, MLP_WIDTH), lambda i: (0, 0)),
                  _resident((1, MLP_WIDTH), lambda i: (0, 0))],
        out_specs=pl.BlockSpec((tm, MLP_WIDTH), row),
        out_shape=jax.ShapeDtypeStruct((S, MLP_WIDTH), BF16),
        scratch_shapes=[pltpu.VMEM((MLP_GROUPS, CHUNK, CHUNK), BF16)],
        compiler_params=pltpu.CompilerParams(dimension_semantics=("arbitrary",), vmem_limit_bytes=VMEM_LIMIT),
        name="gmlp",
    )(u, v, ws, bs_exp, gain)


def _outproj_kernel(h_ref, a_ref, m_ref, ag_ref, wa_ref, wm_ref, pg_ref, out_ref):
    an = _rms(a_ref[...], ag_ref[...]).astype(BF16)
    mix = (jnp.dot(an, wa_ref[...], preferred_element_type=F32)
           + jnp.dot(m_ref[...], wm_ref[...], preferred_element_type=F32))
    out_ref[...] = h_ref[...] + _rms(mix, pg_ref[...])


def _outproj(h, attn, mlpn, ag, wa, wm, pg, tm=1024):
    S, D = h.shape
    row = lambda i: (i, 0)
    fix = lambda i: (0, 0)
    return pl.pallas_call(
        _outproj_kernel,
        grid=(S // tm,),
        in_specs=[pl.BlockSpec((tm, D), row), pl.BlockSpec((tm, ATTN_WIDTH), row), pl.BlockSpec((tm, MLP_WIDTH), row),
                  _resident((1, ATTN_WIDTH), fix), _resident((ATTN_WIDTH, D), fix),
                  _resident((MLP_WIDTH, D), fix), _resident((1, D), fix)],
        out_specs=pl.BlockSpec((tm, D), row),
        out_shape=jax.ShapeDtypeStruct((S, D), F32),
        compiler_params=pltpu.CompilerParams(dimension_semantics=("arbitrary",), vmem_limit_bytes=VMEM_LIMIT),
        name="outproj",
    )(h, attn, mlpn, ag, wa, wm, pg)


HALO = BF16_ROWS


def _ffn_kernel(h_ref, halo_ref, p_ref, g_ref, wup_ref, cw_ref, cb_ref, wd_ref, pg_ref, plg_ref, wpg_ref, wpp_ref,
                out_ref, acc_ref, *, fc):
    dff = wd_ref.shape[0]
    h = h_ref[...]
    hn = _rms(halo_ref[...], g_ref[...])
    hn = jnp.where(pl.program_id(0) == 0, 0.0, hn)
    xn = jnp.concatenate([hn.astype(BF16), _rms(h, g_ref[...]).astype(BF16)], axis=0)

    def conv(lo, n):
        hh = jnp.dot(xn, wup_ref[:, lo:lo + n], preferred_element_type=F32)
        cw = cw_ref[:, lo:lo + n]
        y = cb_ref[:, lo:lo + n] + pltpu.roll(hh, 2, 0) * cw[0:1] + pltpu.roll(hh, 1, 0) * cw[1:2] + hh * cw[2:3]
        return y[HALO:, :]

    for c0 in range(0, dff, fc):
        n = min(fc, dff - c0)
        act = (jax.nn.silu(conv(c0, n)) * conv(dff + c0, n)).astype(BF16)
        part = jnp.dot(act, wd_ref[c0:c0 + n, :], preferred_element_type=F32)
        if c0 == 0:
            acc_ref[...] = part
        else:
            acc_ref[...] += part

    h2 = h + _rms(acc_ref[...], pg_ref[...])
    gate = jax.nn.sigmoid(jnp.dot(_rms(h2, plg_ref[...]).astype(BF16), wpg_ref[...], preferred_element_type=F32))
    proj = jnp.dot(p_ref[...].astype(BF16), wpp_ref[...], preferred_element_type=F32)
    out_ref[...] = h2 + gate * proj


def _ffn(h, p, gain, w_up, conv_w, conv_b, w_down, pg, plg, wpg, wpp, tm=1024, fc=1024):
    S, D = h.shape
    dff = w_down.shape[0]
    P = p.shape[1]
    row = lambda i: (i, 0)
    fix = lambda i: (0, 0)
    halo = lambda i: (jnp.maximum(i * (tm // HALO) - 1, 0), 0)
    return pl.pallas_call(
        functools.partial(_ffn_kernel, fc=fc),
        grid=(S // tm,),
        in_specs=[pl.BlockSpec((tm, D), row), pl.BlockSpec((HALO, D), halo), pl.BlockSpec((tm, P), row),
                  _resident((1, D), fix), _resident((D, 2 * dff), fix), _resident((CONV_WIDTH, 2 * dff), fix),
                  _resident((1, 2 * dff), fix), _resident((dff, D), fix), _resident((1, D), fix),
                  _resident((1, D), fix), _resident((D, D), fix), _resident((P, D), fix)],
        out_specs=pl.BlockSpec((tm, D), row),
        out_shape=jax.ShapeDtypeStruct((S, D), F32),
        scratch_shapes=[pltpu.VMEM((tm, D), F32)],
        compiler_params=pltpu.CompilerParams(dimension_semantics=("arbitrary",), vmem_limit_bytes=VMEM_LIMIT),
        name="convffn_ple",
    )(h, h, p, gain, w_up, conv_w, conv_b, w_down, pg, plg, wpg, wpp)


def _rope_tables(pos):
    half = ROT_DIM // 2
    inv = ROPE_THETA ** (-np.arange(half, dtype=np.float64) / half)
    ang = np.asarray(pos, np.float64)[:, None] * inv[None, :]
    cos, sin = jnp.asarray(np.cos(ang), dtype=F32), jnp.asarray(np.sin(ang), dtype=F32)
    n = ang.shape[0]
    one = jnp.ones((n, HEAD_DIM - ROT_DIM), F32)
    zero = jnp.zeros((n, HEAD_DIM - ROT_DIM), F32)
    zh = jnp.zeros((n, half), F32)
    c = jnp.concatenate([cos, cos, one], axis=1)
    sm = jnp.concatenate([-sin, zh, zero], axis=1)
    sp = jnp.concatenate([zh, sin, zero], axis=1)
    dup = lambda t: jnp.concatenate([t, t], axis=1)
    return dup(c), dup(sm), dup(sp)


def _mask_tables(ncp):
    qo = np.arange(Q_TILE)[None, :]
    rel = (np.arange(2 * ncp) - ncp)[:, None]
    cmp_ok = rel * CMP_STRIDE + CMP_LEN - 1 <= qo
    r = np.arange(WINDOW + Q_TILE)[:, None]
    win_ok = []
    for v in range(WIN_EARLY):
        tq = v * Q_TILE + qo
        win_ok.append((r <= tq) & (r > tq - WINDOW))
    win_ok.append((r - WINDOW <= qo) & (r > qo))
    loc_ok = np.arange(Q_TILE)[:, None] <= qo
    tab = lambda ok: jnp.asarray(np.where(ok, 0.0, NEG), dtype=F32)
    return tab(cmp_ok), tab(np.stack(win_ok)), tab(loc_ok)


def _gate_columns():
    src = np.full((LANES,), -1)
    for g in range(N_GROUPS):
        for br in range(N_BRANCH):
            for r in range(GQA):
                src[g * GATE_SLOT + br * GQA + r] = (g * GQA + r) * N_BRANCH + br
    return src


def _layer(h, p, prm, tabs):
    S, D = h.shape
    rc, rm, rp, crc, crm, crp, cbt, wb, lb = tabs

    sizes = [ATTN_WIDTH] + [KV_WIDTH] * 6 + [N_HEADS * N_BRANCH, MLP_WIDTH, MLP_WIDTH]
    offs = np.concatenate([[0], np.cumsum(sizes)])
    w_in = prm["w_in"]
    seg = lambda k: w_in[:, offs[k]:offs[k + 1]]
    src = _gate_columns()
    wg = jnp.where(jnp.asarray(src >= 0)[None, :], seg(7)[:, np.maximum(src, 0)], 0.0)
    w_cat = jnp.concatenate([seg(0)] + [seg(k) for k in range(1, 7)] + [seg(8), seg(9), wg], axis=1).astype(BF16)

    q, ksa, kwa, vsT, vwT, kcx, vcx, u, v, gates = _inproj(h, prm["pre_mix_g"][None], w_cat, rc, rm, rp,
                                                           prm["gmlp_ln_g"][None], prm["gmlp_ln_b"][None])

    w1 = prm["cmp_w1"].astype(BF16)
    half = CMP_STRIDE * HEAD_DIM
    w1l = jnp.concatenate([w1[:, :half], w1[:, half:]], axis=2).reshape(2, CMP_STRIDE, HEAD_DIM, 2 * CMP_HIDDEN)
    zl = jnp.zeros_like(w1l)
    wboth = jnp.concatenate([jnp.concatenate([w1l, zl], axis=3), jnp.concatenate([zl, w1l], axis=3)], axis=2)
    wboth = wboth.reshape(2, CMP_STRIDE * LANES, 2 * N_GROUPS * CMP_HIDDEN)
    pe = jnp.broadcast_to(prm["cmp_pe"].reshape(2, 1, CMP_LEN * HEAD_DIM),
                          (2, SUBLANES, CMP_LEN * HEAD_DIM)).astype(BF16)
    w2 = jnp.pad(prm["cmp_w2"], ((0, 0), (0, 0), (0, LANES - HEAD_DIM))).astype(BF16)
    b2 = jnp.pad(prm["cmp_b2"], ((0, 0), (0, LANES - HEAD_DIM)))[:, None]
    kc, vcT = _compress(kcx, vcx, wboth, w1, pe, prm["cmp_b1"][:, None], w2, b2, crc, crm, crp)

    attn = _attention(q, gates, ksa, vsT, kwa, vwT, kc, vcT, cbt, wb, lb)

    bs_exp = jnp.repeat(prm["gmlp_bs"].T, HEAD_DIM, axis=1)
    mlpn = _gmlp(u, v, prm["gmlp_ws"], bs_exp, prm["mlp_out_g"][None])

    w_o = prm["w_o"].astype(BF16)
    h = _outproj(h, attn, mlpn, prm["attn_out_g"][None], w_o[:ATTN_WIDTH], w_o[ATTN_WIDTH:], prm["post_mix_g"][None])
    h = _ffn(h, p, prm["pre_ffn_g"][None], prm["w_up"].astype(BF16), prm["conv_w"], prm["conv_b"][None],
             prm["w_down"].astype(BF16), prm["post_ffn_g"][None], prm["ple_norm_g"][None],
             prm["w_ple_gate"].astype(BF16), prm["w_ple_proj"].astype(BF16))
    return h


def kernel(x, p, pre_mix_g, w_in, cmp_pe, cmp_w1, cmp_b1, cmp_w2, cmp_b2, gmlp_ln_g, gmlp_ln_b, gmlp_ws, gmlp_bs,
           attn_out_g, mlp_out_g, w_o, post_mix_g, pre_ffn_g, w_up, conv_w, conv_b, w_down, post_ffn_g,
           ple_norm_g, w_ple_gate, w_ple_proj):
    B, S, D = x.shape
    depth = p.shape[0]
    assert S % (BIAS_WIN * SEL_LEN) == 0 and D % LANES == 0
    nch = S // CMP_STRIDE
    rc, rm, rp = _rope_tables(np.arange(S))
    crc, crm, crp = _rope_tables(np.arange(nch) * CMP_STRIDE + CMP_LEN - 1)
    tabs = (rc, rm, rp, crc, crm, crp) + _mask_tables(nch)
    stacked = dict(pre_mix_g=pre_mix_g, w_in=w_in, cmp_pe=cmp_pe, cmp_w1=cmp_w1, cmp_b1=cmp_b1, cmp_w2=cmp_w2,
                   cmp_b2=cmp_b2, gmlp_ln_g=gmlp_ln_g, gmlp_ln_b=gmlp_ln_b, gmlp_ws=gmlp_ws, gmlp_bs=gmlp_bs,
                   attn_out_g=attn_out_g, mlp_out_g=mlp_out_g, w_o=w_o, post_mix_g=post_mix_g, pre_ffn_g=pre_ffn_g,
                   w_up=w_up, conv_w=conv_w, conv_b=conv_b, w_down=w_down, post_ffn_g=post_ffn_g,
                   ple_norm_g=ple_norm_g, w_ple_gate=w_ple_gate, w_ple_proj=w_ple_proj)
    outs = []
    for bi in range(B):
        h = x[bi]
        for i in range(depth):
            h = _layer(h, p[i, bi], {k: v[i] for k, v in stacked.items()}, tabs)
        outs.append(h)
    return jnp.stack(outs)
```

```python
import functools
import math

import numpy as np
import jax
import jax.numpy as jnp
from jax import lax
from jax.experimental import pallas as pl
from jax.experimental.pallas import tpu as pltpu

F32 = jnp.float32
BF16 = jnp.bfloat16

HEAD_DIM = 64
N_HEADS = 8
N_GROUPS = 2
GQA = N_HEADS // N_GROUPS
N_BRANCH = 3
ATTN_WIDTH = N_HEADS * HEAD_DIM
MLP_GROUPS = 8
MLP_WIDTH = MLP_GROUPS * HEAD_DIM
KV_WIDTH = N_GROUPS * HEAD_DIM
ROT_DIM = HEAD_DIM // 4
ROPE_THETA = 500000.0
CMP_LEN = 32
CMP_STRIDE = 16
CMP_HIDDEN = 256
SEL_LEN = 64
N_SELECT = 16
WINDOW = 512
CHUNK = 128
CONV_WIDTH = 3
NORM_EPS = 1e-6
NEG = -1e30
SCALE = HEAD_DIM ** -0.5
QSCALE = SCALE * math.log2(math.e)

LANES = 128
SUBLANES = 8
BF16_ROWS = 16
Q_TILE = 128
ROWS = GQA * Q_TILE
N_COLS = N_GROUPS * ROWS
KV_TILE = 1024
BIAS_WIN = 128
V_ROWS = HEAD_DIM + BF16_ROWS
CMP_PER_Q = Q_TILE // CMP_STRIDE
SEL_PER_Q = Q_TILE // SEL_LEN
CMP_CLASSES = 8
N_FORCED = 3
GATE_SLOT = 16
WIN_EARLY = WINDOW // Q_TILE
IMP_TAPS = range(-1, SEL_LEN // CMP_STRIDE)
PS_PAD = SUBLANES
VMEM_LIMIT = 56 * 1024 * 1024


def _rms(x, gain):
    return x * lax.rsqrt(jnp.mean(x * x, axis=-1, keepdims=True) + NORM_EPS) * gain


def _rope_slab(x, c, sm, sp):
    return x * c + pltpu.roll(x, LANES - ROT_DIM // 2, 1) * sm + pltpu.roll(x, ROT_DIM // 2, 1) * sp


def _colmax(s):
    return jnp.max(s, axis=0, keepdims=True)


def _tile_heads(x):
    return jnp.concatenate([x] * GQA, axis=1)


def _resident(shape, index_map):
    return pl.BlockSpec(shape, index_map, pipeline_mode=pl.Buffered(1))


def _inproj_kernel(x_ref, g_ref, w_ref, rc_ref, rm_ref, rp_ref, lng_ref, lnb_ref,
                   q_ref, ksa_ref, kwa_ref, vsT_ref, vwT_ref, kcx_ref, vcx_ref, u_ref, v_ref, gate_ref, craw_ref):
    tm = x_ref.shape[0]
    a = _rms(x_ref[...], g_ref[...]).astype(BF16)
    c, sm, sp = rc_ref[...], rm_ref[...], rp_ref[...]
    zq = jnp.dot(a, w_ref[:, 0:ATTN_WIDTH], preferred_element_type=F32)
    for j in range(ATTN_WIDTH // LANES):
        sl = slice(LANES * j, LANES * (j + 1))
        q_ref[:, sl] = (_rope_slab(zq[:, sl], c, sm, sp) * QSCALE).astype(BF16)
    o = ATTN_WIDTH
    zkv = jnp.dot(a, w_ref[:, o:o + 6 * KV_WIDTH], preferred_element_type=F32)
    slab = lambda j: zkv[:, LANES * j:LANES * (j + 1)]

    craw_ref[0] = slab(0)
    craw_ref[1] = slab(1)
    nch = tm // CMP_STRIDE
    for l in range(CMP_STRIDE):
        kcx_ref[l] = craw_ref[0, pl.ds(l, nch, stride=CMP_STRIDE), :].astype(BF16)
        vcx_ref[l] = craw_ref[1, pl.ds(l, nch, stride=CMP_STRIDE), :].astype(BF16)

    pos = pl.program_id(0) * tm + lax.broadcasted_iota(jnp.int32, (tm, BIAS_WIN), 0)
    lane = lax.broadcasted_iota(jnp.int32, (tm, BIAS_WIN), 1)
    ksa_ref[:, 0:LANES] = _rope_slab(slab(2), c, sm, sp).astype(BF16)
    ksa_ref[:, LANES:] = jnp.where((pos // SEL_LEN) % BIAS_WIN == lane, 1.0, 0.0).astype(BF16)
    kwa_ref[...] = _rope_slab(slab(4), c, sm, sp).astype(BF16)
    for k in range(tm // LANES):
        rs = slice(LANES * k, LANES * (k + 1))
        for ref, sl in ((vsT_ref, slab(3)), (vwT_ref, slab(5))):
            t = sl[rs, :].T.astype(BF16)
            for g in range(N_GROUPS):
                ref[g, 0:HEAD_DIM, rs] = t[HEAD_DIM * g:HEAD_DIM * (g + 1)]
                ref[g, HEAD_DIM:, rs] = jnp.ones((BF16_ROWS, LANES), BF16)

    o += 6 * KV_WIDTH
    zu = jnp.dot(a, w_ref[:, o:o + MLP_WIDTH], preferred_element_type=F32)
    u_ref[...] = jax.nn.gelu(zu).astype(BF16)
    o += MLP_WIDTH
    zv = jax.nn.gelu(jnp.dot(a, w_ref[:, o:o + MLP_WIDTH], preferred_element_type=F32))
    mu = jnp.mean(zv, axis=-1, keepdims=True)
    d = zv - mu
    var = jnp.mean(d * d, axis=-1, keepdims=True)
    v_ref[...] = (d * lax.rsqrt(var + NORM_EPS) * lng_ref[...] + lnb_ref[...]).astype(BF16)
    o += MLP_WIDTH
    zg = jnp.dot(a, w_ref[:, o:o + LANES], preferred_element_type=F32)
    gate_ref[...] = jax.nn.sigmoid(zg)


def _inproj(h, gain, w, rc, rm, rp, lng, lnb, tm=1024):
    S, D = h.shape
    N = w.shape[1]
    nch = S // CMP_STRIDE
    row = lambda i: (i, 0)
    fix = lambda i: (0, 0)
    chunked = lambda i: (0, i, 0)
    bf = lambda *shape: jax.ShapeDtypeStruct(shape, BF16)
    return pl.pallas_call(
        _inproj_kernel,
        grid=(S // tm,),
        in_specs=[pl.BlockSpec((tm, D), row), _resident((1, D), fix), _resident((D, N), fix),
                  pl.BlockSpec((tm, LANES), row), pl.BlockSpec((tm, LANES), row), pl.BlockSpec((tm, LANES), row),
                  _resident((1, MLP_WIDTH), fix), _resident((1, MLP_WIDTH), fix)],
        out_specs=[pl.BlockSpec((tm, ATTN_WIDTH), row), pl.BlockSpec((tm, 2 * LANES), row),
                   pl.BlockSpec((tm, LANES), row), pl.BlockSpec((N_GROUPS, V_ROWS, tm), lambda i: (0, 0, i)),
                   pl.BlockSpec((N_GROUPS, V_ROWS, tm), lambda i: (0, 0, i)),
                   pl.BlockSpec((CMP_STRIDE, tm // CMP_STRIDE, LANES), chunked),
                   pl.BlockSpec((CMP_STRIDE, tm // CMP_STRIDE, LANES), chunked),
                   pl.BlockSpec((tm, MLP_WIDTH), row), pl.BlockSpec((tm, MLP_WIDTH), row),
                   pl.BlockSpec((tm, LANES), row)],
        out_shape=[bf(S, ATTN_WIDTH), bf(S, 2 * LANES), bf(S, LANES), bf(N_GROUPS, V_ROWS, S), bf(N_GROUPS, V_ROWS, S),
                   bf(CMP_STRIDE, nch, LANES), bf(CMP_STRIDE, nch, LANES), bf(S, MLP_WIDTH), bf(S, MLP_WIDTH),
                   jax.ShapeDtypeStruct((S, LANES), F32)],
        scratch_shapes=[pltpu.VMEM((2, tm, LANES), F32)],
        compiler_params=pltpu.CompilerParams(dimension_semantics=("arbitrary",), vmem_limit_bytes=VMEM_LIMIT),
        name="inproj",
    )(h, gain, w, rc, rm, rp, lng, lnb)


def _compress_kernel(xk_ref, xv_ref, wb_ref, w1_ref, pe_ref, b1_ref, w2_ref, b2_ref, rc_ref, rm_ref, rp_ref,
                     kc_ref, vcT_ref):
    nch = xk_ref.shape[1]

    def run(x_ref):
        x = jnp.concatenate([x_ref[l] for l in range(CMP_STRIDE)], axis=1)
        c = jnp.dot(x, wb_ref[0], preferred_element_type=F32)
        cvec = jnp.dot(pe_ref[0], w1_ref[0], preferred_element_type=F32)[0:1] + b1_ref[0]
        outs = []
        for g in range(N_GROUPS):
            cg = c[:, 2 * CMP_HIDDEN * g:2 * CMP_HIDDEN * (g + 1)]
            h = jax.nn.gelu(cg[:, :CMP_HIDDEN] + pltpu.roll(cg[:, CMP_HIDDEN:], nch - 1, 0) + cvec)
            outs.append(jnp.dot(h.astype(BF16), w2_ref[0], preferred_element_type=F32) + b2_ref[0])
        return outs

    @pl.when(pl.program_id(0) == 0)
    def _():
        o = run(xk_ref)
        shared = o[0]
        for g in range(1, N_GROUPS):
            shared = shared + pltpu.roll(o[g], HEAD_DIM * g, 1)
        kc_ref[...] = _rope_slab(shared, rc_ref[...], rm_ref[...], rp_ref[...]).astype(BF16)

    @pl.when(pl.program_id(0) == 1)
    def _():
        for g, o in enumerate(run(xv_ref)):
            for k in range(nch // LANES):
                rs = slice(LANES * k, LANES * (k + 1))
                vcT_ref[g, :, rs] = o[rs, :].T[0:HEAD_DIM].astype(BF16)


def _compress(kcx, vcx, wboth, w1, pe, b1, w2, b2, rc, rm, rp):
    _, nch, _ = kcx.shape
    cw = CMP_STRIDE * HEAD_DIM
    kv3 = lambda k: (k, 0, 0)
    fix3 = lambda k: (0, 0, 0)
    fix2 = lambda k: (0, 0)
    return pl.pallas_call(
        _compress_kernel,
        grid=(2,),
        in_specs=[_resident((CMP_STRIDE, nch, LANES), fix3), _resident((CMP_STRIDE, nch, LANES), fix3),
                  pl.BlockSpec((1, CMP_STRIDE * LANES, 2 * N_GROUPS * CMP_HIDDEN), kv3),
                  pl.BlockSpec((1, 2 * cw, CMP_HIDDEN), kv3), pl.BlockSpec((1, SUBLANES, 2 * cw), kv3),
                  pl.BlockSpec((1, 1, CMP_HIDDEN), kv3), pl.BlockSpec((1, CMP_HIDDEN, LANES), kv3),
                  pl.BlockSpec((1, 1, LANES), kv3),
                  _resident((nch, LANES), fix2), _resident((nch, LANES), fix2), _resident((nch, LANES), fix2)],
        out_specs=[pl.BlockSpec((nch, LANES), fix2), pl.BlockSpec((N_GROUPS, HEAD_DIM, nch), fix3)],
        out_shape=[jax.ShapeDtypeStruct((nch, LANES), BF16),
                   jax.ShapeDtypeStruct((N_GROUPS, HEAD_DIM, nch), BF16)],
        compiler_params=pltpu.CompilerParams(dimension_semantics=("arbitrary",), vmem_limit_bytes=VMEM_LIMIT),
        name="compress",
    )(kcx, vcx, wboth, w1, pe, b1, w2, b2, rc, rm, rp)


def _attn_kernel(q_ref, gate_ref, ksa_ref, vsT_ref, kwa_ref, vwT_ref, kc_ref, vcT_ref, cbt_ref, wb_ref, lb_ref,
                 out_ref, qaug_ref, sbuf_ref, acc_ref, ma_ref, ms_ref, ps_ref, oc_ref, ow_ref, sel_ref,
                 *, ns, ncp):
    b = pl.program_id(0)
    t0 = pl.multiple_of(b * Q_TILE, Q_TILE)
    gcols = lambda g: slice(ROWS * g, ROWS * (g + 1))
    gq = lambda g: slice(Q_TILE * g, Q_TILE * (g + 1))
    per_group = lambda f: jnp.concatenate([f(g) for g in range(N_GROUPS)], axis=1)

    qf = q_ref[...].astype(F32)
    parts = []
    for j in range(N_HEADS // 2):
        t = qf[:, LANES * j:LANES * (j + 1)].T
        parts += [t[0:HEAD_DIM], t[HEAD_DIM:]]
    qT = jnp.concatenate(parts, axis=1).astype(BF16)
    zeros = jnp.zeros((HEAD_DIM, ROWS), BF16)
    qgT = jnp.concatenate(
        [jnp.concatenate([qT[:, gcols(g)] if gg == g else zeros for gg in range(N_GROUPS)], axis=1)
         for g in range(N_GROUPS)], axis=0)

    step = ncp // CMP_CLASSES
    cls = (b * CMP_PER_Q + CMP_PER_Q - 2) // step
    for k in range(CMP_CLASSES):
        @pl.when(cls == k)
        def _(nk=(k + 1) * step):
            cb = _tile_heads(cbt_ref[pl.ds(pl.multiple_of(ncp - b * CMP_PER_Q, SUBLANES), nk), :])
            s = jnp.dot(kc_ref[0:nk, :], qgT, preferred_element_type=F32) + per_group(lambda g: cb)
            m = _colmax(s)
            e = jnp.exp2(s - m)
            inv = jnp.where(m > NEG * 0.5, 1.0 / jnp.maximum(jnp.sum(e, axis=0, keepdims=True), 1e-30), 0.0)
            p = e * inv
            for g in range(N_GROUPS):
                pg = p[:, gcols(g)]
                oc_ref[:, gcols(g)] = jnp.dot(vcT_ref[g, :, 0:nk], pg.astype(BF16), preferred_element_type=F32)
                ps = pg[:, 0:Q_TILE]
                for r in range(1, GQA):
                    ps = ps + pg[:, r * Q_TILE:(r + 1) * Q_TILE]
                ps_ref[g, 0:PS_PAD, :] = jnp.zeros((PS_PAD, Q_TILE), F32)
                ps_ref[g, PS_PAD:PS_PAD + nk, :] = ps
                if nk < ncp:
                    ps_ref[g, PS_PAD + nk:, :] = jnp.zeros((ncp - nk, Q_TILE), F32)

    ratio = SEL_LEN // CMP_STRIDE
    imp = per_group(lambda g: sum(ps_ref[g, pl.ds(PS_PAD + d, ns, stride=ratio), :] for d in IMP_TAPS))

    wk = WINDOW + Q_TILE
    ws = pl.multiple_of(jnp.maximum(t0 - WINDOW, 0), Q_TILE)
    wb = _tile_heads(wb_ref[jnp.minimum(b, WIN_EARLY)])
    s = jnp.dot(kwa_ref[pl.ds(ws, wk), :], qgT, preferred_element_type=F32) + per_group(lambda g: wb)
    pw = jnp.exp2(s - _colmax(s)).astype(BF16)
    for g in range(N_GROUPS):
        ow = jnp.dot(vwT_ref[g, :, pl.ds(ws, wk)], pw[:, gcols(g)], preferred_element_type=F32)
        ow_ref[:, gcols(g)] = ow[0:HEAD_DIM] / jnp.maximum(ow[HEAD_DIM:HEAD_DIM + 1], 1e-30)

    lb = _tile_heads(lb_ref[...])
    s = (jnp.dot(ksa_ref[pl.ds(t0, Q_TILE), 0:LANES], qgT, preferred_element_type=F32)
         + per_group(lambda g: lb))
    m0 = _colmax(s)
    p0 = jnp.exp2(s - m0).astype(BF16)
    for g in range(N_GROUPS):
        acc_ref[g] = jnp.dot(vsT_ref[g, :, pl.ds(t0, Q_TILE)], p0[:, gcols(g)], preferred_element_type=F32)
    ma_ref[...] = m0
    ms_ref[...] = m0

    nq = N_GROUPS * Q_TILE
    blk = lax.broadcasted_iota(jnp.int32, (ns, nq), 0)
    cur = (t0 + (lax.broadcasted_iota(jnp.int32, (ns, nq), 1) & (Q_TILE - 1))) // SEL_LEN
    valid = blk <= cur
    forced = valid & ((blk == 0) | (blk == cur) | (blk == cur - 1))
    rank0 = jnp.where(valid & jnp.logical_not(forced), imp, NEG)
    blk_f = blk.astype(F32)
    taken = -(2.0 ** 126)

    def rank(break_ties):
        score = rank0
        for _ in range(N_SELECT - N_FORCED):
            mx = _colmax(score)
            if break_ties:
                idx = jnp.min(jnp.where(score == mx, blk_f, float(ns)), axis=0, keepdims=True)
                hit = blk_f == idx
            else:
                hit = score == mx
            score = jnp.where(hit, taken, score)
        return (score == taken) & (rank0 > NEG * 0.5)

    picked = rank(False)
    count = jnp.sum(jnp.where(picked, 1.0, 0.0), axis=0, keepdims=True)
    want = jnp.clip(cur[0:1] - (N_FORCED - 1), 0, N_SELECT - N_FORCED).astype(F32)
    sel_ref[...] = jnp.where(forced | picked, 1.0, 0.0)

    @pl.when(jnp.logical_not(jnp.all(count == want)))
    def _():
        sel_ref[...] = jnp.where(forced | rank(True), 1.0, 0.0)

    sel = sel_ref[...] > 0.5
    bias = jnp.where(sel & (blk < b * SEL_PER_Q), 0.0, NEG).astype(BF16)
    for w in range(ns // BIAS_WIN):
        qaug_ref[w, 0:2 * HEAD_DIM, :] = qgT
        qaug_ref[w, 2 * HEAD_DIM:, :] = per_group(
            lambda g: _tile_heads(bias[BIAS_WIN * w:BIAS_WIN * (w + 1), gq(g)]))

    n_tiles = jnp.maximum((t0 + KV_TILE - 1) // KV_TILE, 1)
    tiles_per_win = BIAS_WIN * SEL_LEN // KV_TILE

    def scores(t, slot):
        start = pl.multiple_of(t * KV_TILE, KV_TILE)
        s = jnp.dot(ksa_ref[pl.ds(start, KV_TILE), :], qaug_ref[t // tiles_per_win],
                    preferred_element_type=F32)
        sbuf_ref[slot] = s
        ms_ref[...] = jnp.maximum(ms_ref[...], _colmax(s))

    def consume(t, slot, m_s, rows=KV_TILE):
        start = pl.multiple_of(t * KV_TILE, KV_TILE)
        alpha = jnp.exp2(ma_ref[...] - m_s)
        pp = jnp.exp2(sbuf_ref[slot, 0:rows, :] - m_s).astype(BF16)
        for g in range(N_GROUPS):
            acc_ref[g] = alpha[:, gcols(g)] * acc_ref[g] + jnp.dot(
                vsT_ref[g, :, pl.ds(start, rows)], pp[:, gcols(g)], preferred_element_type=F32)
        ma_ref[...] = m_s

    def step_(t, slot):
        m_s = ms_ref[...]
        scores(t + 1, 1 - slot)
        consume(t, slot, m_s)

    scores(0, 0)

    def body(i, carry):
        step_(2 * i, 0)
        step_(2 * i + 1, 1)
        return carry

    n_steps = n_tiles - 1
    lax.fori_loop(0, n_steps // 2, body, 0)

    @pl.when(n_steps % 2 == 1)
    def _():
        step_(n_steps - 1, 0)

    last_chunks = jnp.maximum((t0 - n_steps * KV_TILE) // Q_TILE, 1)
    for k in range(1, KV_TILE // Q_TILE + 1):
        @pl.when(last_chunks == k)
        def _(rows=k * Q_TILE):
            consume(n_steps, n_steps % 2, ms_ref[...], rows)

    owT = ow_ref[...]
    osT = per_group(lambda g: acc_ref[g, 0:HEAD_DIM, :] / jnp.maximum(acc_ref[g, HEAD_DIM:HEAD_DIM + 1, :], 1e-30))

    gall = gate_ref[...].T
    row = lambda i: gall[i:i + 1]
    gt = lambda br: jnp.concatenate([row(g * GATE_SLOT + br * GQA + r)
                                     for g in range(N_GROUPS) for r in range(GQA)], axis=1)
    oT = gt(0) * oc_ref[...] + gt(1) * osT + gt(2) * owT
    for j in range(N_HEADS // 2):
        pair = jnp.concatenate([oT[:, Q_TILE * 2 * j:Q_TILE * (2 * j + 1)],
                                oT[:, Q_TILE * (2 * j + 1):Q_TILE * (2 * j + 2)]], axis=0)
        out_ref[:, LANES * j:LANES * (j + 1)] = pair.T


def _attention(q, gates, ksa, vsT, kwa, vwT, kc, vcT, cbt, wb, lb):
    S = q.shape[0]
    nqb = S // Q_TILE
    ncp = kc.shape[0]
    ns = S // SEL_LEN
    assert KV_WIDTH == LANES
    row = lambda b: (b, 0)
    fix = lambda b: (0, 0)
    fix3 = lambda b: (0, 0, 0)
    kernel = functools.partial(_attn_kernel, ns=ns, ncp=ncp)
    return pl.pallas_call(
        kernel,
        grid=(nqb,),
        in_specs=[pl.BlockSpec((Q_TILE, ATTN_WIDTH), row), pl.BlockSpec((Q_TILE, LANES), row),
                  _resident((S, 2 * LANES), fix), _resident((N_GROUPS, V_ROWS, S), fix3),
                  _resident((S, LANES), fix), _resident((N_GROUPS, V_ROWS, S), fix3),
                  _resident((ncp, LANES), fix), _resident((N_GROUPS, HEAD_DIM, ncp), fix3),
                  _resident((2 * ncp, Q_TILE), fix),
                  _resident((WIN_EARLY + 1, WINDOW + Q_TILE, Q_TILE), fix3),
                  _resident((Q_TILE, Q_TILE), fix)],
        out_specs=pl.BlockSpec((Q_TILE, ATTN_WIDTH), row),
        out_shape=jax.ShapeDtypeStruct((S, ATTN_WIDTH), F32),
        scratch_shapes=[pltpu.VMEM((ns // BIAS_WIN, 2 * LANES, N_COLS), BF16),
                        pltpu.VMEM((2, KV_TILE, N_COLS), F32),
                        pltpu.VMEM((N_GROUPS, V_ROWS, ROWS), F32),
                        pltpu.VMEM((1, N_COLS), F32), pltpu.VMEM((1, N_COLS), F32),
                        pltpu.VMEM((N_GROUPS, PS_PAD + ncp, Q_TILE), F32),
                        pltpu.VMEM((HEAD_DIM, N_COLS), F32),
                        pltpu.VMEM((HEAD_DIM, N_COLS), F32),
                        pltpu.VMEM((ns, N_GROUPS * Q_TILE), F32)],
        compiler_params=pltpu.CompilerParams(dimension_semantics=("arbitrary",), vmem_limit_bytes=VMEM_LIMIT),
        name="nsa_attention",
    )(q, gates, ksa, vsT, kwa, vwT, kc, vcT, cbt, wb, lb)


def _gmlp_kernel(u_ref, v_ref, ws_ref, bs_ref, g_ref, out_ref, wsm_ref, *, chunks):
    @pl.when(pl.program_id(0) == 0)
    def _():
        r = lax.broadcasted_iota(jnp.int32, (CHUNK, CHUNK), 0)
        c = lax.broadcasted_iota(jnp.int32, (CHUNK, CHUNK), 1)
        for g in range(MLP_GROUPS):
            wsm_ref[g] = jnp.where(c <= r, ws_ref[g], 0.0).astype(BF16)

    lane = lax.broadcasted_iota(jnp.int32, (CHUNK, LANES), 1)
    left = lane < HEAD_DIM
    for ci in range(chunks):
        rows = slice(ci * CHUNK, (ci + 1) * CHUNK)
        parts = []
        for pr in range(MLP_GROUPS // 2):
            vp = v_ref[rows, pr * LANES:(pr + 1) * LANES]
            va = jnp.where(left, vp, jnp.zeros_like(vp))
            vb = jnp.where(left, jnp.zeros_like(vp), vp)
            parts.append(jnp.dot(wsm_ref[2 * pr], va, preferred_element_type=F32)
                         + jnp.dot(wsm_ref[2 * pr + 1], vb, preferred_element_type=F32))
        mixed = jnp.concatenate(parts, axis=1) + bs_ref[...]
        y = u_ref[rows, :].astype(F32) * mixed
        out_ref[rows, :] = _rms(y, g_ref[...]).astype(BF16)


def _outproj_kernel(h_ref, a_ref, u_ref, v_ref, ws_ref, bs_ref, mg_ref, ag_ref, wa_ref, wm_ref, pg_ref,
                    out_ref, m_ref, wsm_ref):
    _gmlp_kernel(u_ref, v_ref, ws_ref, bs_ref, mg_ref, m_ref, wsm_ref, chunks=h_ref.shape[0] // CHUNK)
    an = _rms(a_ref[...], ag_ref[...]).astype(BF16)
    mix = (jnp.dot(an, wa_ref[...], preferred_element_type=F32)
           + jnp.dot(m_ref[...], wm_ref[...], preferred_element_type=F32))
    out_ref[...] = h_ref[...] + _rms(mix, pg_ref[...])


def _outproj(h, attn, u, v, ws, bs_exp, mg, ag, wa, wm, pg, tm=1024):
    S, D = h.shape
    row = lambda i: (i, 0)
    fix = lambda i: (0, 0)
    return pl.pallas_call(
        _outproj_kernel,
        grid=(S // tm,),
        in_specs=[pl.BlockSpec((tm, D), row), pl.BlockSpec((tm, ATTN_WIDTH), row),
                  pl.BlockSpec((tm, MLP_WIDTH), row), pl.BlockSpec((tm, MLP_WIDTH), row),
                  _resident((MLP_GROUPS, CHUNK, CHUNK), lambda i: (0, 0, 0)), _resident((CHUNK, MLP_WIDTH), fix),
                  _resident((1, MLP_WIDTH), fix),
                  _resident((1, ATTN_WIDTH), fix), _resident((ATTN_WIDTH, D), fix),
                  _resident((MLP_WIDTH, D), fix), _resident((1, D), fix)],
        out_specs=pl.BlockSpec((tm, D), row),
        out_shape=jax.ShapeDtypeStruct((S, D), F32),
        scratch_shapes=[pltpu.VMEM((tm, MLP_WIDTH), BF16), pltpu.VMEM((MLP_GROUPS, CHUNK, CHUNK), BF16)],
        compiler_params=pltpu.CompilerParams(dimension_semantics=("arbitrary",), vmem_limit_bytes=VMEM_LIMIT),
        name="gmlp_outproj",
    )(h, attn, u, v, ws, bs_exp, mg, ag, wa, wm, pg)


HALO = BF16_ROWS


def _ffn_kernel(h_ref, halo_ref, p_ref, g_ref, wup_ref, cw_ref, cb_ref, wd_ref, pg_ref, plg_ref, wpg_ref, wpp_ref,
                out_ref, acc_ref, *, fc):
    dff = wd_ref.shape[0]
    h = h_ref[...]
    hn = _rms(halo_ref[...], g_ref[...])
    hn = jnp.where(pl.program_id(0) == 0, 0.0, hn)
    xn = jnp.concatenate([hn.astype(BF16), _rms(h, g_ref[...]).astype(BF16)], axis=0)

    def conv(lo, n):
        hh = jnp.dot(xn, wup_ref[:, lo:lo + n], preferred_element_type=F32)
        cw = cw_ref[:, lo:lo + n]
        y = cb_ref[:, lo:lo + n] + pltpu.roll(hh, 2, 0) * cw[0:1] + pltpu.roll(hh, 1, 0) * cw[1:2] + hh * cw[2:3]
        return y[HALO:, :]

    for c0 in range(0, dff, fc):
        n = min(fc, dff - c0)
        act = (jax.nn.silu(conv(c0, n)) * conv(dff + c0, n)).astype(BF16)
        part = jnp.dot(act, wd_ref[c0:c0 + n, :], preferred_element_type=F32)
        if c0 == 0:
            acc_ref[...] = part
        else:
            acc_ref[...] += part

    h2 = h + _rms(acc_ref[...], pg_ref[...])
    gate = jax.nn.sigmoid(jnp.dot(_rms(h2, plg_ref[...]).astype(BF16), wpg_ref[...], preferred_element_type=F32))
    proj = jnp.dot(p_ref[...].astype(BF16), wpp_ref[...], preferred_element_type=F32)
    out_ref[...] = h2 + gate * proj


def _ffn(h, p, gain, w_up, conv_w, conv_b, w_down, pg, plg, wpg, wpp, tm=1024, fc=1024):
    S, D = h.shape
    dff = w_down.shape[0]
    P = p.shape[1]
    row = lambda i: (i, 0)
    fix = lambda i: (0, 0)
    halo = lambda i: (jnp.maximum(i * (tm // HALO) - 1, 0), 0)
    return pl.pallas_call(
        functools.partial(_ffn_kernel, fc=fc),
        grid=(S // tm,),
        in_specs=[pl.BlockSpec((tm, D), row), pl.BlockSpec((HALO, D), halo), pl.BlockSpec((tm, P), row),
                  _resident((1, D), fix), _resident((D, 2 * dff), fix), _resident((CONV_WIDTH, 2 * dff), fix),
                  _resident((1, 2 * dff), fix), _resident((dff, D), fix), _resident((1, D), fix),
                  _resident((1, D), fix), _resident((D, D), fix), _resident((P, D), fix)],
        out_specs=pl.BlockSpec((tm, D), row),
        out_shape=jax.ShapeDtypeStruct((S, D), F32),
        scratch_shapes=[pltpu.VMEM((tm, D), F32)],
        compiler_params=pltpu.CompilerParams(dimension_semantics=("arbitrary",), vmem_limit_bytes=VMEM_LIMIT),
        name="convffn_ple",
    )(h, h, p, gain, w_up, conv_w, conv_b, w_down, pg, plg, wpg, wpp)


def _rope_tables(pos):
    half = ROT_DIM // 2
    inv = ROPE_THETA ** (-np.arange(half, dtype=np.float64) / half)
    ang = np.asarray(pos, np.float64)[:, None] * inv[None, :]
    cos, sin = jnp.asarray(np.cos(ang), dtype=F32), jnp.asarray(np.sin(ang), dtype=F32)
    n = ang.shape[0]
    one = jnp.ones((n, HEAD_DIM - ROT_DIM), F32)
    zero = jnp.zeros((n, HEAD_DIM - ROT_DIM), F32)
    zh = jnp.zeros((n, half), F32)
    c = jnp.concatenate([cos, cos, one], axis=1)
    sm = jnp.concatenate([-sin, zh, zero], axis=1)
    sp = jnp.concatenate([zh, sin, zero], axis=1)
    dup = lambda t: jnp.concatenate([t, t], axis=1)
    return dup(c), dup(sm), dup(sp)


def _mask_tables(ncp):
    qo = np.arange(Q_TILE)[None, :]
    rel = (np.arange(2 * ncp) - ncp)[:, None]
    cmp_ok = rel * CMP_STRIDE + CMP_LEN - 1 <= qo
    r = np.arange(WINDOW + Q_TILE)[:, None]
    win_ok = []
    for v in range(WIN_EARLY):
        tq = v * Q_TILE + qo
        win_ok.append((r <= tq) & (r > tq - WINDOW))
    win_ok.append((r - WINDOW <= qo) & (r > qo))
    loc_ok = np.arange(Q_TILE)[:, None] <= qo
    tab = lambda ok: jnp.asarray(np.where(ok, 0.0, NEG), dtype=F32)
    return tab(cmp_ok), tab(np.stack(win_ok)), tab(loc_ok)


def _gate_columns():
    src = np.full((LANES,), -1)
    for g in range(N_GROUPS):
        for br in range(N_BRANCH):
            for r in range(GQA):
                src[g * GATE_SLOT + br * GQA + r] = (g * GQA + r) * N_BRANCH + br
    return src


def _layer(h, p, prm, tabs):
    S, D = h.shape
    rc, rm, rp, crc, crm, crp, cbt, wb, lb = tabs

    sizes = [ATTN_WIDTH] + [KV_WIDTH] * 6 + [N_HEADS * N_BRANCH, MLP_WIDTH, MLP_WIDTH]
    offs = np.concatenate([[0], np.cumsum(sizes)])
    w_in = prm["w_in"]
    seg = lambda k: w_in[:, offs[k]:offs[k + 1]]
    src = _gate_columns()
    wg = jnp.where(jnp.asarray(src >= 0)[None, :], seg(7)[:, np.maximum(src, 0)], 0.0)
    w_cat = jnp.concatenate([seg(0)] + [seg(k) for k in range(1, 7)] + [seg(8), seg(9), wg], axis=1).astype(BF16)

    q, ksa, kwa, vsT, vwT, kcx, vcx, u, v, gates = _inproj(h, prm["pre_mix_g"][None], w_cat, rc, rm, rp,
                                                           prm["gmlp_ln_g"][None], prm["gmlp_ln_b"][None])

    w1 = prm["cmp_w1"].astype(BF16)
    half = CMP_STRIDE * HEAD_DIM
    w1l = jnp.concatenate([w1[:, :half], w1[:, half:]], axis=2).reshape(2, CMP_STRIDE, HEAD_DIM, 2 * CMP_HIDDEN)
    zl = jnp.zeros_like(w1l)
    wboth = jnp.concatenate([jnp.concatenate([w1l, zl], axis=3), jnp.concatenate([zl, w1l], axis=3)], axis=2)
    wboth = wboth.reshape(2, CMP_STRIDE * LANES, 2 * N_GROUPS * CMP_HIDDEN)
    pe = jnp.broadcast_to(prm["cmp_pe"].reshape(2, 1, CMP_LEN * HEAD_DIM),
                          (2, SUBLANES, CMP_LEN * HEAD_DIM)).astype(BF16)
    w2 = jnp.pad(prm["cmp_w2"], ((0, 0), (0, 0), (0, LANES - HEAD_DIM))).astype(BF16)
    b2 = jnp.pad(prm["cmp_b2"], ((0, 0), (0, LANES - HEAD_DIM)))[:, None]
    kc, vcT = _compress(kcx, vcx, wboth, w1, pe, prm["cmp_b1"][:, None], w2, b2, crc, crm, crp)

    attn = _attention(q, gates, ksa, vsT, kwa, vwT, kc, vcT, cbt, wb, lb)

    bs_exp = jnp.repeat(prm["gmlp_bs"].T, HEAD_DIM, axis=1)
    w_o = prm["w_o"].astype(BF16)
    h = _outproj(h, attn, u, v, prm["gmlp_ws"], bs_exp, prm["mlp_out_g"][None], prm["attn_out_g"][None],
                 w_o[:ATTN_WIDTH], w_o[ATTN_WIDTH:], prm["post_mix_g"][None])
    h = _ffn(h, p, prm["pre_ffn_g"][None], prm["w_up"].astype(BF16), prm["conv_w"], prm["conv_b"][None],
             prm["w_down"].astype(BF16), prm["post_ffn_g"][None], prm["ple_norm_g"][None],
             prm["w_ple_gate"].astype(BF16), prm["w_ple_proj"].astype(BF16))
    return h


def kernel(x, p, pre_mix_g, w_in, cmp_pe, cmp_w1, cmp_b1, cmp_w2, cmp_b2, gmlp_ln_g, gmlp_ln_b, gmlp_ws, gmlp_bs,
           attn_out_g, mlp_out_g, w_o, post_mix_g, pre_ffn_g, w_up, conv_w, conv_b, w_down, post_ffn_g,
           ple_norm_g, w_ple_gate, w_ple_proj):
    B, S, D = x.shape
    depth = p.shape[0]
    assert S % (BIAS_WIN * SEL_LEN) == 0 and D % LANES == 0
    nch = S // CMP_STRIDE
    rc, rm, rp = _rope_tables(np.arange(S))
    crc, crm, crp = _rope_tables(np.arange(nch) * CMP_STRIDE + CMP_LEN - 1)
    tabs = (rc, rm, rp, crc, crm, crp) + _mask_tables(nch)
    stacked = dict(pre_mix_g=pre_mix_g, w_in=w_in, cmp_pe=cmp_pe, cmp_w1=cmp_w1, cmp_b1=cmp_b1, cmp_w2=cmp_w2,
                   cmp_b2=cmp_b2, gmlp_ln_g=gmlp_ln_g, gmlp_ln_b=gmlp_ln_b, gmlp_ws=gmlp_ws, gmlp_bs=gmlp_bs,
                   attn_out_g=attn_out_g, mlp_out_g=mlp_out_g, w_o=w_o, post_mix_g=post_mix_g, pre_ffn_g=pre_ffn_g,
                   w_up=w_up, conv_w=conv_w, conv_b=conv_b, w_down=w_down, post_ffn_g=post_ffn_g,
                   ple_norm_g=ple_norm_g, w_ple_gate=w_ple_gate, w_ple_proj=w_ple_proj)
    outs = []
    for bi in range(B):
        h = x[bi]
        for i in range(depth):
            h = _layer(h, p[i, bi], {k: v[i] for k, v in stacked.items()}, tabs)
        outs.append(h)
    return jnp.stack(outs)
```
